```python
import jax, jax.numpy as jnp
from jax import lax
import numpy as np

D_MODEL = 1024
BATCH = 1
SEQ = 16384
DEPTH = 4
DEC_BATCH = 32
DEC_SEQ = 32
PAST_LEN = 2048

CHUNK = 64
N_EVEN = (DEPTH + 1) // 2
N_ODD = DEPTH // 2
D_FF = 2816
GROUP_DIM = 64
W_A = 512
W_B = 512
K_A = 3
K_B = 31
H_C = 12
DH = 64
W_C = H_C * DH
N_POOL = 4
W_D = N_POOL * GROUP_DIM
POOL_WINDOWS = (2, 4, 8, 16)
POOL_MAX = 16
D_IN_EVEN = 3 * W_A + 2 * W_B
D_IN_ODD = 3 * W_C + H_C + W_D
D_MIX_EVEN = W_A + W_B
D_MIX_ODD = W_C + W_D
Q_BLOCK = 128
EPS = 1e-6
NEG_INF = -1e30

kernel_name = 'hybrid_streaming_encoder_step'


def rms_norm(x, g):
    xf = x.astype(jnp.float32)
    y = xf * lax.rsqrt(jnp.mean(xf * xf, axis=-1, keepdims=True) + EPS)
    return (y * g.astype(jnp.float32)).astype(x.dtype)


def layer_norm(x, g, b):
    xf = x.astype(jnp.float32)
    mu = jnp.mean(xf, axis=-1, keepdims=True)
    var = jnp.mean(jnp.square(xf - mu), axis=-1, keepdims=True)
    y = (xf - mu) * lax.rsqrt(var + EPS) * g.astype(jnp.float32) + b.astype(jnp.float32)
    return y.astype(x.dtype)


def swiglu_ffn(x, w_in, w_out):
    gate, up = jnp.split(x @ w_in, 2, axis=-1)
    return (jax.nn.silu(gate) * up) @ w_out


def causal_dwconv(u, prev, w):
    K, C = w.shape
    ext = jnp.concatenate([prev, u], axis=1)
    y = lax.conv_general_dilated(ext, w[:, None, :].astype(ext.dtype), window_strides=(1,), padding='VALID',
                                 dimension_numbers=('NWC', 'WIO', 'NWC'), feature_group_count=C)
    return y, ext[:, ext.shape[1] - (K - 1):]


def even_mixer(h, prev_a, prev_b, w_in, conv_a_w, conv_b_w, conv_b_bias, ln_g, ln_b, w_out):
    p = h @ w_in
    b_gate, c_gate, x_t, glu_val, glu_gate = jnp.split(p, [W_A, 2 * W_A, 3 * W_A, 3 * W_A + W_B], axis=-1)
    ya_conv, new_a = causal_dwconv(c_gate * x_t, prev_a, conv_a_w)
    ya = b_gate * ya_conv
    ub = glu_val * jax.nn.sigmoid(glu_gate)
    yb_conv, new_b = causal_dwconv(ub, prev_b, conv_b_w)
    yb = jax.nn.silu(layer_norm(yb_conv + conv_b_bias, ln_g, ln_b))
    y = jnp.concatenate([ya, yb], axis=-1) @ w_out
    return y, new_a, new_b


def fox_attention(q, k_all, v_all, logf_all):
    B, L, H, D = q.shape
    S = k_all.shape[1]
    n_past = S - L
    F_k = jnp.transpose(jnp.cumsum(logf_all.astype(jnp.float32), axis=1), (0, 2, 1))
    F_q = F_k[:, :, n_past:]
    qb = min(Q_BLOCK, L)
    nb = L // qb
    k_pos = jnp.arange(S)
    q_pos = n_past + jnp.arange(L)
    scale = DH ** -0.5

    def block(args):
        q_blk, fq_blk, qpos_blk = args
        s = jnp.einsum('bqhd,bkhd->bhqk', q_blk, k_all, preferred_element_type=jnp.float32) * scale
        s = s + fq_blk[..., None] - F_k[:, :, None, :]
        s = jnp.where(k_pos[None, :] <= qpos_blk[:, None], s, NEG_INF)
        p = jax.nn.softmax(s, axis=-1).astype(v_all.dtype)
        return jnp.einsum('bhqk,bkhd->bqhd', p, v_all)

    q_blocks = q.reshape(B, nb, qb, H, D).transpose(1, 0, 2, 3, 4)
    fq_blocks = F_q.reshape(B, H, nb, qb).transpose(2, 0, 1, 3)
    qpos_blocks = q_pos.reshape(nb, qb)
    o = lax.map(block, (q_blocks, fq_blocks, qpos_blocks))
    return o.transpose(1, 0, 2, 3, 4).reshape(B, L, H * D)


def pool_mixer(u, prev, pos0, pool_w, pool_scale):
    B, L, _ = u.shape
    P = POOL_MAX - 1
    ext = jnp.concatenate([prev, u], axis=1)
    cs = jnp.pad(jnp.cumsum(ext.astype(jnp.float32), axis=1), ((0, 0), (1, 0), (0, 0)))
    end = cs[:, P + 1:]
    pos = pos0 + jnp.arange(L)
    groups = []
    for g, w in enumerate(POOL_WINDOWS):
        sl = slice(g * GROUP_DIM, (g + 1) * GROUP_DIM)
        win = end[:, :, sl] - cs[:, P + 1 - w:P + 1 - w + L, sl]
        cnt = jnp.minimum(w, pos + 1).astype(jnp.float32)[None, :, None]
        groups.append(win / cnt)
    pooled = jnp.stack(groups, axis=2)
    z = (pooled - u.reshape(B, L, N_POOL, GROUP_DIM).astype(jnp.float32)).astype(u.dtype)
    y = jnp.einsum('blgc,gcd->blgd', z, pool_w).reshape(B, L, W_D) * pool_scale
    return y, ext[:, ext.shape[1] - P:]


def odd_mixer(h, past, prev_pool, pos0, w_in, b_f, pool_w, pool_scale, w_out):
    B, L, _ = h.shape
    p = h @ w_in
    q, k, v, f_logit, u = jnp.split(p, [W_C, 2 * W_C, 3 * W_C, 3 * W_C + H_C], axis=-1)
    q = q.reshape(B, L, H_C, DH)
    k = k.reshape(B, L, H_C, DH)
    v = v.reshape(B, L, H_C, DH)
    logf = jax.nn.log_sigmoid((f_logit + b_f).astype(jnp.float32)).astype(h.dtype)
    if past is None:
        k_all, v_all, f_all = k, v, logf
    else:
        k_all = jnp.concatenate([past[0], k], axis=1)
        v_all = jnp.concatenate([past[1], v], axis=1)
        f_all = jnp.concatenate([past[2], logf], axis=1)
    o = fox_attention(q, k_all, v_all, f_all)
    yd, new_pool = pool_mixer(u, prev_pool, pos0, pool_w, pool_scale)
    y = jnp.concatenate([o, yd], axis=-1) @ w_out
    return y, k, v, logf, new_pool


def trunk(x, pos0, prev_a, prev_b, kv_past, prev_pool, W):
    outs_a, outs_b, outs_k, outs_v, outs_f, outs_p = [], [], [], [], [], []
    for i in range(DEPTH):
        j = i // 2
        x = x + 0.5 * swiglu_ffn(rms_norm(x, W['norm_ffn1'][i]), W['w_ffn1_in'][i], W['w_ffn1_out'][i])
        h = rms_norm(x, W['norm_mix'][i])
        if i % 2 == 0:
            y, sa, sb = even_mixer(h, prev_a[j], prev_b[j], W['w_in_even'][j], W['conv_a_w'][j], W['conv_b_w'][j],
                                   W['conv_b_bias'][j], W['ln_b_gain'][j], W['ln_b_bias'][j], W['w_out_even'][j])
            outs_a.append(sa)
            outs_b.append(sb)
        else:
            past = None if kv_past is None else (kv_past[0][j], kv_past[1][j], kv_past[2][j])
            y, k, v, f, sp = odd_mixer(h, past, prev_pool[j], pos0, W['w_in_odd'][j], W['b_forget'][j],
                                       W['pool_w'][j], W['pool_scale'][j], W['w_out_odd'][j])
            outs_k.append(k)
            outs_v.append(v)
            outs_f.append(f)
            outs_p.append(sp)
        x = x + y
        x = x + 0.5 * swiglu_ffn(rms_norm(x, W['norm_ffn2'][i]), W['w_ffn2_in'][i], W['w_ffn2_out'][i])
    y_out = rms_norm(x, W['norm_final'])
    return y_out, jnp.stack(outs_a), jnp.stack(outs_b), jnp.stack(outs_k), jnp.stack(outs_v), jnp.stack(outs_f), jnp.stack(outs_p)


def setup_inputs(seed: int = 0) -> dict:
    key = jax.random.key(seed)
    ks = jax.random.split(key, 32)

    def nrm(k, shape, s):
        return jax.random.normal(k, shape, jnp.float32) * s

    def gain(k, shape):
        return 1.0 + 0.05 * jax.random.normal(k, shape, jnp.float32)

    return {
        'x_prompt': nrm(ks[0], (BATCH, SEQ, D_MODEL), 1.0),
        'x_sample': nrm(ks[1], (DEC_BATCH, DEC_SEQ, D_MODEL), 1.0),
        'state_conv_a': nrm(ks[2], (N_EVEN, DEC_BATCH, K_A - 1, W_A), 1.0),
        'state_conv_b': nrm(ks[3], (N_EVEN, DEC_BATCH, K_B - 1, W_B), 1.0),
        'cache_k': nrm(ks[4], (N_ODD, DEC_BATCH, PAST_LEN, H_C, DH), 1.0),
        'cache_v': nrm(ks[5], (N_ODD, DEC_BATCH, PAST_LEN, H_C, DH), 1.0),
        'cache_logf': jax.nn.log_sigmoid(3.0 + nrm(ks[6], (N_ODD, DEC_BATCH, PAST_LEN, H_C), 1.0)),
        'state_pool': nrm(ks[7], (N_ODD, DEC_BATCH, POOL_MAX - 1, W_D), 1.0),
        'norm_ffn1': gain(ks[8], (DEPTH, D_MODEL)),
        'w_ffn1_in': nrm(ks[9], (DEPTH, D_MODEL, 2 * D_FF), D_MODEL ** -0.5),
        'w_ffn1_out': nrm(ks[10], (DEPTH, D_FF, D_MODEL), D_FF ** -0.5),
        'norm_mix': gain(ks[11], (DEPTH, D_MODEL)),
        'w_in_even': nrm(ks[12], (N_EVEN, D_MODEL, D_IN_EVEN), D_MODEL ** -0.5),
        'conv_a_w': nrm(ks[13], (N_EVEN, K_A, W_A), K_A ** -0.5),
        'conv_b_w': nrm(ks[14], (N_EVEN, K_B, W_B), K_B ** -0.5),
        'conv_b_bias': nrm(ks[15], (N_EVEN, W_B), 0.02),
        'ln_b_gain': gain(ks[16], (N_EVEN, W_B)),
        'ln_b_bias': nrm(ks[17], (N_EVEN, W_B), 0.02),
        'w_out_even': nrm(ks[18], (N_EVEN, D_MIX_EVEN, D_MODEL), D_MIX_EVEN ** -0.5),
        'w_in_odd': nrm(ks[19], (N_ODD, D_MODEL, D_IN_ODD), D_MODEL ** -0.5),
        'b_forget': jnp.linspace(1.0, 5.0, H_C, dtype=jnp.float32)[None, :] + nrm(ks[20], (N_ODD, H_C), 0.1),
        'pool_w': nrm(ks[21], (N_ODD, N_POOL, GROUP_DIM, GROUP_DIM), GROUP_DIM ** -0.5),
        'pool_scale': gain(ks[22], (N_ODD, W_D)),
        'w_out_odd': nrm(ks[23], (N_ODD, D_MIX_ODD, D_MODEL), D_MIX_ODD ** -0.5),
        'norm_ffn2': gain(ks[24], (DEPTH, D_MODEL)),
        'w_ffn2_in': nrm(ks[25], (DEPTH, D_MODEL, 2 * D_FF), D_MODEL ** -0.5),
        'w_ffn2_out': nrm(ks[26], (DEPTH, D_FF, D_MODEL), D_FF ** -0.5),
        'norm_final': gain(ks[27], (D_MODEL,)),
    }


def reference(x_prompt, x_sample, state_conv_a, state_conv_b, cache_k, cache_v, cache_logf, state_pool,
              norm_ffn1, w_ffn1_in, w_ffn1_out, norm_mix, w_in_even, conv_a_w, conv_b_w, conv_b_bias,
              ln_b_gain, ln_b_bias, w_out_even, w_in_odd, b_forget, pool_w, pool_scale, w_out_odd,
              norm_ffn2, w_ffn2_in, w_ffn2_out, norm_final):
    W = {
        'norm_ffn1': norm_ffn1, 'w_ffn1_in': w_ffn1_in, 'w_ffn1_out': w_ffn1_out, 'norm_mix': norm_mix,
        'w_in_even': w_in_even, 'conv_a_w': conv_a_w, 'conv_b_w': conv_b_w, 'conv_b_bias': conv_b_bias,
        'ln_b_gain': ln_b_gain, 'ln_b_bias': ln_b_bias, 'w_out_even': w_out_even,
        'w_in_odd': w_in_odd, 'b_forget': b_forget, 'pool_w': pool_w, 'pool_scale': pool_scale,
        'w_out_odd': w_out_odd, 'norm_ffn2': norm_ffn2, 'w_ffn2_in': w_ffn2_in, 'w_ffn2_out': w_ffn2_out,
        'norm_final': norm_final,
    }
    B = x_prompt.shape[0]
    dt = x_prompt.dtype
    zero_a = jnp.zeros((N_EVEN, B, K_A - 1, W_A), dt)
    zero_b = jnp.zeros((N_EVEN, B, K_B - 1, W_B), dt)
    zero_pool = jnp.zeros((N_ODD, B, POOL_MAX - 1, W_D), dt)
    y_prompt, p_conv_a, p_conv_b, p_k, p_v, p_logf, p_pool = trunk(
        x_prompt, 0, zero_a, zero_b, None, zero_pool, W)
    y_sample, s_conv_a, s_conv_b, s_k, s_v, s_logf, s_pool = trunk(
        x_sample, PAST_LEN, state_conv_a, state_conv_b, (cache_k, cache_v, cache_logf), state_pool, W)
    return (y_prompt, y_sample, p_conv_a, p_conv_b, p_k, p_v, p_logf, p_pool,
            s_conv_a, s_conv_b, s_k, s_v, s_logf, s_pool)
```

```python
import functools

import numpy as np
import jax
import jax.numpy as jnp
from jax import lax
from jax.experimental import pallas as pl
from jax.experimental.pallas import tpu as pltpu

F32 = jnp.float32
BF16 = jnp.bfloat16

EPS = 1e-6
NEG_INF = -1e30
GROUP_DIM = 64
HEAD_DIM = 64
N_HEADS = 12
W_A = 512
W_B = 512
K_A = 3
K_B = 31
W_C = N_HEADS * HEAD_DIM
W_D = 256
POOL_WINDOWS = (2, 4, 8, 16)
POOL_MAX = 16
LANES = 128
HALO_A = 8
HALO_B = 32
HALO_P = 16
VMEM_LIMIT = 56 * 1024 * 1024

ROW_TILE = 512
ATT_TILE = 512
FF_CHUNK = 256
SCAN_BLOCK = 128


def _rms(x, g):
    return (x * lax.rsqrt(jnp.mean(x * x, axis=-1, keepdims=True) + EPS)) * g


def _dot(a, b):
    return jnp.dot(a, b, preferred_element_type=F32)


def _dot_nt(a, b):
    return lax.dot_general(a, b, (((1,), (1,)), ((), ())), preferred_element_type=F32)


def _dot_exact(a, b):
    return lax.dot_general(a, b, (((1,), (0,)), ((), ())), precision=lax.Precision.HIGHEST,
                           preferred_element_type=F32)


def _params(semantics):
    return pltpu.CompilerParams(dimension_semantics=semantics, vmem_limit_bytes=VMEM_LIMIT)


def _ffn_kernel(*refs, n_chunks, final):
    if final:
        x_ref, g_ref, wg_ref, wu_ref, wo_ref, gf_ref, o_ref, h_scr, acc_scr = refs
    else:
        x_ref, g_ref, wg_ref, wu_ref, wo_ref, o_ref, h_scr, acc_scr = refs
    h_scr[...] = _rms(x_ref[...], g_ref[...]).astype(BF16)
    for c in range(n_chunks):
        sl = slice(c * FF_CHUNK, (c + 1) * FF_CHUNK)
        hb = h_scr[...]
        gate = _dot(hb, wg_ref[:, sl])
        up = _dot(hb, wu_ref[:, sl])
        act = ((gate * jax.nn.sigmoid(gate)) * up).astype(BF16)
        part = _dot(act, wo_ref[sl, :])
        if c == 0:
            acc_scr[...] = part
        else:
            acc_scr[...] += part
    y = x_ref[...] + 0.5 * acc_scr[...]
    if final:
        y = _rms(y, gf_ref[...])
    o_ref[...] = y


def _ffn(x, g, w_in, w_out, layer, g_final=None):
    rows, d = x.shape
    d_ff = w_out.shape[1]
    tm = min(ROW_TILE, rows)
    final = g_final is not None
    in_specs = [
        pl.BlockSpec((tm, d), lambda i: (i, 0)),
        pl.BlockSpec((None, 1, d), lambda i: (layer, 0, 0)),
        pl.BlockSpec((None, d, d_ff), lambda i: (layer, 0, 0)),
        pl.BlockSpec((None, d, d_ff), lambda i: (layer, 0, 1)),
        pl.BlockSpec((None, d_ff, d), lambda i: (layer, 0, 0)),
    ]
    args = [x, g, w_in, w_in, w_out]
    if final:
        in_specs.append(pl.BlockSpec((1, d), lambda i: (0, 0)))
        args.append(g_final)
    return pl.pallas_call(
        functools.partial(_ffn_kernel, n_chunks=d_ff // FF_CHUNK, final=final),
        grid=(rows // tm,),
        in_specs=in_specs,
        out_specs=pl.BlockSpec((tm, d), lambda i: (i, 0)),
        out_shape=jax.ShapeDtypeStruct((rows, d), F32),
        scratch_shapes=[pltpu.VMEM((tm, d), BF16), pltpu.VMEM((tm, d), F32)],
        compiler_params=_params(("arbitrary",)),
        name="ffn",
    )(*args)


def _even_kernel(x_ref, g_ref, win_ref, wa_ref, wb_ref, bb_ref, lng_ref, lnb_ref, wout_ref, pa_ref, pb_ref,
                 o_ref, na_ref, nb_ref, ext_a, ext_b, *, nb, lt, n_l):
    j = pl.program_id(1)

    @pl.when(j == 0)
    def _():
        ext_a[:, HALO_A - (K_A - 1):HALO_A, :] = pa_ref[...]
        ext_b[:, HALO_B - (K_B - 1):HALO_B, :] = pb_ref[...]

    if n_l > 1:
        @pl.when(j > 0)
        def _():
            ext_a[:, 0:HALO_A, :] = ext_a[:, lt:lt + HALO_A, :]
            ext_b[:, 0:HALO_B, :] = ext_b[:, lt:lt + HALO_B, :]

    x = x_ref[...]
    h = _rms(x, g_ref[...]).astype(BF16)

    def proj(i, width):
        return _dot(h, win_ref[:, i * width:(i + 1) * width])

    ext_a[:, HALO_A:HALO_A + lt, :] = (proj(1, W_A) * proj(2, W_A)).reshape(nb, lt, W_A)
    ext_b[:, HALO_B:HALO_B + lt, :] = (proj(3, W_B) * jax.nn.sigmoid(proj(4, W_B))).reshape(nb, lt, W_B)

    def conv(ext, w_ref, taps, halo):
        base = halo - (taps - 1)
        acc = w_ref[0:1, :] * ext[:, base:base + lt, :]
        for k in range(1, taps):
            acc = acc + w_ref[k:k + 1, :] * ext[:, base + k:base + k + lt, :]
        return acc.reshape(nb * lt, acc.shape[-1])

    ya = proj(0, W_A) * conv(ext_a, wa_ref, K_A, HALO_A)
    yc = conv(ext_b, wb_ref, K_B, HALO_B) + bb_ref[...]
    mu = jnp.mean(yc, axis=-1, keepdims=True)
    dev = yc - mu
    var = jnp.mean(dev * dev, axis=-1, keepdims=True)
    yn = dev * lax.rsqrt(var + EPS) * lng_ref[...] + lnb_ref[...]
    yb = yn * jax.nn.sigmoid(yn)
    y = _dot(ya.astype(BF16), wout_ref[0:W_A, :]) + _dot(yb.astype(BF16), wout_ref[W_A:W_A + W_B, :])
    o_ref[...] = x + y

    @pl.when(j == n_l - 1)
    def _():
        na_ref[...] = ext_a[:, HALO_A + lt - (K_A - 1):HALO_A + lt, :]
        nb_ref[...] = ext_b[:, HALO_B + lt - (K_B - 1):HALO_B + lt, :]


def _seq_tiling(batch, length):
    if length >= ROW_TILE:
        return 1, ROW_TILE, length // ROW_TILE
    nb = max(1, min(batch, ROW_TILE // length))
    return nb, length, 1


def _even(x, batch, length, g, w_in, conv_a, conv_b, bias_b, ln_g, ln_b, w_out, prev_a, prev_b, layer, j):
    rows, d = x.shape
    nb, lt, n_l = _seq_tiling(batch, length)
    tm = nb * lt
    d_in = w_in.shape[-1]
    row_map = lambda b, t: (b * n_l + t, 0)
    lay3 = lambda b, t: (j, 0, 0)
    seq_map = lambda b, t: (b, 0, 0)
    return pl.pallas_call(
        functools.partial(_even_kernel, nb=nb, lt=lt, n_l=n_l),
        grid=(batch // nb, n_l),
        in_specs=[
            pl.BlockSpec((tm, d), row_map),
            pl.BlockSpec((None, 1, d), lambda b, t: (layer, 0, 0)),
            pl.BlockSpec((None, d, d_in), lay3),
            pl.BlockSpec((None, K_A, W_A), lay3),
            pl.BlockSpec((None, K_B, W_B), lay3),
            pl.BlockSpec((None, 1, W_B), lay3),
            pl.BlockSpec((None, 1, W_B), lay3),
            pl.BlockSpec((None, 1, W_B), lay3),
            pl.BlockSpec((None, W_A + W_B, d), lay3),
            pl.BlockSpec((nb, K_A - 1, W_A), seq_map),
            pl.BlockSpec((nb, K_B - 1, W_B), seq_map),
        ],
        out_specs=[
            pl.BlockSpec((tm, d), row_map),
            pl.BlockSpec((nb, K_A - 1, W_A), seq_map),
            pl.BlockSpec((nb, K_B - 1, W_B), seq_map),
        ],
        out_shape=[
            jax.ShapeDtypeStruct((rows, d), F32),
            jax.ShapeDtypeStruct((batch, K_A - 1, W_A), F32),
            jax.ShapeDtypeStruct((batch, K_B - 1, W_B), F32),
        ],
        scratch_shapes=[pltpu.VMEM((nb, HALO_A + lt, W_A), F32), pltpu.VMEM((nb, HALO_B + lt, W_B), F32)],
        compiler_params=_params(("arbitrary", "arbitrary")),
        name="even",
    )(x, g, w_in, conv_a, conv_b, bias_b, ln_g, ln_b, w_out, prev_a, prev_b)


def _odd_in_kernel(*refs, lt, n_l, with_sum):
    x_ref, g_ref, w_ref, bf_ref = refs[:4]
    q_ref, kb_ref, vb_ref, k_ref, v_ref, lf_ref, lfp_ref, u_ref = refs[4:12]
    if with_sum:
        fcol_ref, frow_ref, carry = refs[12:]
    h = _rms(x_ref[...], g_ref[...]).astype(BF16)
    q_ref[...] = (_dot(h, w_ref[:, 0:W_C]) * (HEAD_DIM ** -0.5)).astype(BF16)
    k = _dot(h, w_ref[:, W_C:2 * W_C])
    k_ref[...] = k
    kb_ref[...] = k.astype(BF16)
    v = _dot(h, w_ref[:, 2 * W_C:3 * W_C])
    v_ref[...] = v
    vb_ref[...] = v.astype(BF16)
    z = _dot(h, w_ref[:, 3 * W_C:3 * W_C + LANES]) + bf_ref[...]
    lf = jnp.minimum(z, 0.0) - jnp.log1p(jnp.exp(-jnp.abs(z)))
    lfp_ref[...] = lf
    lf_ref[...] = lf[:, :N_HEADS]
    u_ref[...] = _dot(h, w_ref[:, 3 * W_C + LANES:3 * W_C + LANES + W_D])
    if with_sum:
        t = pl.program_id(0)

        @pl.when(t == 0)
        def _():
            carry[...] = jnp.zeros_like(carry)

        r = lax.broadcasted_iota(jnp.int32, (lt, lt), 0)
        c = lax.broadcasted_iota(jnp.int32, (lt, lt), 1)
        f = _dot_exact((c <= r).astype(F32), lf) + carry[...]
        carry[...] = f[lt - 1:lt, :]
        fcol_ref[...] = f
        frow_ref[...] = f.T[:16, :]


def _odd_in(x, g, w, b_f, layer, j, with_sum):
    rows, d = x.shape
    tm = min(ROW_TILE, rows)
    n = rows // tm
    d_in = w.shape[-1]
    row = lambda i: (i, 0)
    lay3 = lambda i: (j, 0, 0)
    out_specs = [
        pl.BlockSpec((tm, W_C), row), pl.BlockSpec((tm, W_C), row), pl.BlockSpec((tm, W_C), row),
        pl.BlockSpec((tm, W_C), row), pl.BlockSpec((tm, W_C), row),
        pl.BlockSpec((tm, N_HEADS), row), pl.BlockSpec((tm, LANES), row), pl.BlockSpec((tm, W_D), row),
    ]
    out_shape = [
        jax.ShapeDtypeStruct((rows, W_C), BF16), jax.ShapeDtypeStruct((rows, W_C), BF16),
        jax.ShapeDtypeStruct((rows, W_C), BF16),
        jax.ShapeDtypeStruct((rows, W_C), F32), jax.ShapeDtypeStruct((rows, W_C), F32),
        jax.ShapeDtypeStruct((rows, N_HEADS), F32), jax.ShapeDtypeStruct((rows, LANES), F32),
        jax.ShapeDtypeStruct((rows, W_D), F32),
    ]
    scratch = []
    if with_sum:
        out_specs += [pl.BlockSpec((tm, LANES), row), pl.BlockSpec((16, tm), lambda i: (0, i))]
        out_shape += [jax.ShapeDtypeStruct((rows, LANES), F32), jax.ShapeDtypeStruct((16, rows), F32)]
        scratch = [pltpu.VMEM((1, LANES), F32)]
    return pl.pallas_call(
        functools.partial(_odd_in_kernel, lt=tm, n_l=n, with_sum=with_sum),
        grid=(n,),
        in_specs=[
            pl.BlockSpec((tm, d), row),
            pl.BlockSpec((None, 1, d), lambda i: (layer, 0, 0)),
            pl.BlockSpec((None, d, d_in), lay3),
            pl.BlockSpec((None, 1, LANES), lay3),
        ],
        out_specs=out_specs,
        out_shape=out_shape,
        scratch_shapes=scratch,
        compiler_params=_params(("arbitrary",)),
        name="odd_in",
    )(x, g, w, b_f)


def _flash_kernel(qi_tab, ki_tab, q_ref, k_ref, v_ref, fcol_ref, frow_ref, o_ref, m_scr, l_scr, acc_scr, fq_scr, *, tq):
    p = pl.program_id(0)
    t = pl.program_id(1)
    qi = qi_tab[t]
    ki = ki_tab[t]
    lane = lax.broadcasted_iota(jnp.int32, (1, LANES), 1)

    @pl.when(ki == 0)
    def _():
        m_scr[...] = jnp.full_like(m_scr, NEG_INF)
        l_scr[...] = jnp.zeros_like(l_scr)
        acc_scr[...] = jnp.zeros_like(acc_scr)
        for hh in range(2):
            col = jnp.sum(jnp.where(lane == 2 * p + hh, fcol_ref[...], 0.0), axis=1, keepdims=True)
            fq_scr[hh] = jnp.broadcast_to(col, (tq, LANES))

    def step(masked):
        q = q_ref[...]
        k = k_ref[...]
        v = v_ref[...]
        for hh in range(2):
            qm = jnp.where((lane >= HEAD_DIM) == (hh == 1), q, jnp.zeros_like(q))
            s = _dot_nt(qm, k)
            fk = frow_ref[pl.ds(2 * p + hh, 1), :]
            s = s + fq_scr[hh][:, 0:1] - fk
            if masked:
                row = lax.broadcasted_iota(jnp.int32, s.shape, 0)
                colv = lax.broadcasted_iota(jnp.int32, s.shape, 1)
                s = jnp.where(colv <= row, s, NEG_INF)
            m_prev = m_scr[hh][:, 0:1]
            m_new = jnp.maximum(m_prev, jnp.max(s, axis=1, keepdims=True))
            alpha = jnp.exp(m_prev - m_new)
            e = jnp.exp(s - m_new)
            l_scr[hh] = alpha * l_scr[hh] + jnp.sum(e, axis=1, keepdims=True)
            acc_scr[hh] = alpha * acc_scr[hh] + _dot(e.astype(BF16), v)
            m_scr[hh] = jnp.broadcast_to(m_new, (tq, LANES))

    @pl.when(ki < qi)
    def _():
        step(False)

    @pl.when(ki == qi)
    def _():
        step(True)
        o0 = acc_scr[0] / l_scr[0]
        o1 = acc_scr[1] / l_scr[1]
        o_ref[...] = jnp.where(lane < HEAD_DIM, o0, o1).astype(BF16)


def _flash(q, k, v, fcol, frow):
    rows = q.shape[0]
    tq = min(ATT_TILE, rows)
    nq = rows // tq
    qi_tab = np.concatenate([np.full(i + 1, i, np.int32) for i in range(nq)])
    ki_tab = np.concatenate([np.arange(i + 1, dtype=np.int32) for i in range(nq)])
    grid_spec = pltpu.PrefetchScalarGridSpec(
        num_scalar_prefetch=2,
        grid=(N_HEADS // 2, qi_tab.shape[0]),
        in_specs=[
            pl.BlockSpec((tq, LANES), lambda p, t, qt, kt: (qt[t], p)),
            pl.BlockSpec((tq, LANES), lambda p, t, qt, kt: (kt[t], p)),
            pl.BlockSpec((tq, LANES), lambda p, t, qt, kt: (kt[t], p)),
            pl.BlockSpec((tq, LANES), lambda p, t, qt, kt: (qt[t], 0)),
            pl.BlockSpec((16, tq), lambda p, t, qt, kt: (0, kt[t])),
        ],
        out_specs=pl.BlockSpec((tq, LANES), lambda p, t, qt, kt: (qt[t], p)),
        scratch_shapes=[pltpu.VMEM((2, tq, LANES), F32)] * 4,
    )
    return pl.pallas_call(
        functools.partial(_flash_kernel, tq=tq),
        grid_spec=grid_spec,
        out_shape=jax.ShapeDtypeStruct((rows, W_C), BF16),
        compiler_params=_params(("arbitrary", "arbitrary")),
        name="flash",
    )(jnp.asarray(qi_tab), jnp.asarray(ki_tab), q, k, v, fcol, frow)


def _decode_kernel(q_ref, kn_ref, vn_ref, ck_ref, cv_ref, cl_ref, lfp_ref, o_ref, kall, vall, lfa, gcol, grow,
                   *, past, length, s_pad):
    p = pl.program_id(1)
    lane = lax.broadcasted_iota(jnp.int32, (1, LANES), 1)
    n_blk = s_pad // SCAN_BLOCK

    @pl.when(p == 0)
    def _():
        lfa[...] = jnp.zeros_like(lfa)
        lfa[0:past, 0:N_HEADS] = cl_ref[...]
        lfa[past:past + length, :] = lfp_ref[...]
        r = lax.broadcasted_iota(jnp.int32, (SCAN_BLOCK, SCAN_BLOCK), 0)
        c = lax.broadcasted_iota(jnp.int32, (SCAN_BLOCK, SCAN_BLOCK), 1)
        upper = (c > r).astype(F32)
        carry = jnp.zeros((1, LANES), F32)
        for b in reversed(range(n_blk)):
            blk = lfa[b * SCAN_BLOCK:(b + 1) * SCAN_BLOCK, :]
            g = _dot_exact(upper, blk) + carry
            carry = carry + jnp.sum(blk, axis=0, keepdims=True)
            gcol[b * SCAN_BLOCK:(b + 1) * SCAN_BLOCK, :] = g
            grow[:, b * SCAN_BLOCK:(b + 1) * SCAN_BLOCK] = g.T
        kall[past + length:s_pad, :] = jnp.zeros((s_pad - past - length, LANES), BF16)
        vall[past + length:s_pad, :] = jnp.zeros((s_pad - past - length, LANES), BF16)

    kall[0:past, :] = ck_ref[...].astype(BF16)
    vall[0:past, :] = cv_ref[...].astype(BF16)
    kall[past:past + length, :] = kn_ref[...]
    vall[past:past + length, :] = vn_ref[...]
    q = q_ref[...]
    row = lax.broadcasted_iota(jnp.int32, (length, s_pad), 0)
    col = lax.broadcasted_iota(jnp.int32, (length, s_pad), 1)
    visible = col <= row + past
    outs = []
    for hh in range(2):
        qm = jnp.where((lane >= HEAD_DIM) == (hh == 1), q, jnp.zeros_like(q))
        s = _dot_nt(qm, kall[...])
        gq = jnp.sum(jnp.where(lane == 2 * p + hh, gcol[past:past + length, :], 0.0), axis=1, keepdims=True)
        gk = grow[pl.ds(2 * p + hh, 1), :]
        s = jnp.where(visible, s + (gk - gq), NEG_INF)
        m = jnp.max(s, axis=1, keepdims=True)
        e = jnp.exp(s - m)
        l = jnp.sum(e, axis=1, keepdims=True)
        outs.append(_dot(e.astype(BF16), vall[...]) / l)
    o_ref[...] = jnp.where(lane < HEAD_DIM, outs[0], outs[1]).astype(BF16)


def _decode(q, kn, vn, cache_k, cache_v, cache_logf, lfp, batch, length, j):
    rows = q.shape[0]
    past = cache_k.shape[2]
    s_pad = past + LANES
    ck = cache_k.reshape(cache_k.shape[0] * batch, past, W_C)
    cv = cache_v.reshape(cache_v.shape[0] * batch, past, W_C)
    cl = cache_logf.reshape(cache_logf.shape[0] * batch, past, N_HEADS)
    new_map = lambda b, p: (b, p)
    cache_map = lambda b, p: (j * batch + b, 0, p)
    return pl.pallas_call(
        functools.partial(_decode_kernel, past=past, length=length, s_pad=s_pad),
        grid=(batch, N_HEADS // 2),
        in_specs=[
            pl.BlockSpec((length, LANES), new_map),
            pl.BlockSpec((length, LANES), new_map),
            pl.BlockSpec((length, LANES), new_map),
            pl.BlockSpec((None, past, LANES), cache_map),
            pl.BlockSpec((None, past, LANES), cache_map),
            pl.BlockSpec((None, past, N_HEADS), lambda b, p: (j * batch + b, 0, 0)),
            pl.BlockSpec((length, LANES), lambda b, p: (b, 0)),
        ],
        out_specs=pl.BlockSpec((length, LANES), new_map),
        out_shape=jax.ShapeDtypeStruct((rows, W_C), BF16),
        scratch_shapes=[
            pltpu.VMEM((s_pad, LANES), BF16), pltpu.VMEM((s_pad, LANES), BF16),
            pltpu.VMEM((s_pad, LANES), F32), pltpu.VMEM((s_pad, LANES), F32), pltpu.VMEM((LANES, s_pad), F32),
        ],
        compiler_params=_params(("arbitrary", "arbitrary")),
        name="decode",
    )(q, kn, vn, ck, cv, cl, lfp)


def _odd_out_kernel(o_ref, u_ref, x_ref, pp_ref, pw_ref, ps_ref, wout_ref, y_ref, np_ref, ext, *, nb, lt, n_l, pos0):
    j = pl.program_id(1)
    hist = POOL_MAX - 1

    @pl.when(j == 0)
    def _():
        ext[:, HALO_P - hist:HALO_P, :] = pp_ref[...]

    if n_l > 1:
        @pl.when(j > 0)
        def _():
            ext[:, 0:HALO_P, :] = ext[:, lt:lt + HALO_P, :]

    u = u_ref[...].reshape(nb, lt, W_D)
    ext[:, HALO_P:HALO_P + lt, :] = u
    group = lax.broadcasted_iota(jnp.int32, (nb, lt, W_D), 2) // GROUP_DIM
    acc = u
    win = None
    for back in range(1, POOL_MAX):
        acc = acc + ext[:, HALO_P - back:HALO_P - back + lt, :]
        if back + 1 in POOL_WINDOWS:
            g = POOL_WINDOWS.index(back + 1)
            win = acc if win is None else jnp.where(group >= g, acc, win)
    width = jnp.left_shift(2, group)
    pos = pos0 + j * lt + lax.broadcasted_iota(jnp.int32, (nb, lt, W_D), 1)
    cnt = jnp.minimum(width, pos + 1).astype(F32)
    z = (win / cnt - u).reshape(nb * lt, W_D)
    yd = _dot(z.astype(BF16), pw_ref[...]) * ps_ref[...]
    y = _dot(o_ref[...], wout_ref[0:W_C, :]) + _dot(yd.astype(BF16), wout_ref[W_C:W_C + W_D, :])
    y_ref[...] = x_ref[...] + y

    @pl.when(j == n_l - 1)
    def _():
        np_ref[...] = ext[:, HALO_P + lt - hist:HALO_P + lt, :]


def _odd_out(o, u, x, batch, length, pos0, prev_pool, pool_w, pool_scale, w_out, j):
    rows, d = x.shape
    nb, lt, n_l = _seq_tiling(batch, length)
    tm = nb * lt
    row_map = lambda b, t: (b * n_l + t, 0)
    lay3 = lambda b, t: (j, 0, 0)
    seq_map = lambda b, t: (b, 0, 0)
    return pl.pallas_call(
        functools.partial(_odd_out_kernel, nb=nb, lt=lt, n_l=n_l, pos0=pos0),
        grid=(batch // nb, n_l),
        in_specs=[
            pl.BlockSpec((tm, W_C), row_map),
            pl.BlockSpec((tm, W_D), row_map),
            pl.BlockSpec((tm, d), row_map),
            pl.BlockSpec((nb, POOL_MAX - 1, W_D), seq_map),
            pl.BlockSpec((None, W_D, W_D), lay3),
            pl.BlockSpec((None, 1, W_D), lay3),
            pl.BlockSpec((None, W_C + W_D, d), lay3),
        ],
        out_specs=[pl.BlockSpec((tm, d), row_map), pl.BlockSpec((nb, POOL_MAX - 1, W_D), seq_map)],
        out_shape=[jax.ShapeDtypeStruct((rows, d), F32), jax.ShapeDtypeStruct((batch, POOL_MAX - 1, W_D), F32)],
        scratch_shapes=[pltpu.VMEM((nb, HALO_P + lt, W_D), F32)],
        compiler_params=_params(("arbitrary", "arbitrary")),
        name="odd_out",
    )(o, u, x, prev_pool, pool_w, pool_scale, w_out)


def _trunk(x3, prev_a, prev_b, cache, prev_pool, W):
    batch, length, d = x3.shape
    depth = W['w_ffn1_in'].shape[0]
    x = x3.reshape(batch * length, d)
    pos0 = 0 if cache is None else cache[0].shape[2]
    outs = {n: [] for n in 'abkvfp'}
    for i in range(depth):
        j = i // 2
        x = _ffn(x, W['norm_ffn1'], W['w_ffn1_in'], W['w_ffn1_out'], i)
        if i % 2 == 0:
            x, sa, sb = _even(x, batch, length, W['norm_mix'], W['w_in_even'], W['conv_a_w'], W['conv_b_w'],
                              W['conv_b_bias'], W['ln_b_gain'], W['ln_b_bias'], W['w_out_even'],
                              prev_a[j], prev_b[j], i, j)
            outs['a'].append(sa)
            outs['b'].append(sb)
        else:
            res = _odd_in(x, W['norm_mix'], W['w_in_odd'], W['b_forget'], i, j, with_sum=cache is None)
            q, kb, vb, k, v, lf, lfp, u = res[:8]
            if cache is None:
                o = _flash(q, kb, vb, res[8], res[9])
            else:
                o = _decode(q, kb, vb, cache[0], cache[1], cache[2], lfp, batch, length, j)
            x, sp = _odd_out(o, u, x, batch, length, pos0, prev_pool[j], W['pool_w'], W['pool_scale'],
                             W['w_out_odd'], j)
            outs['k'].append(k.reshape(batch, length, N_HEADS, HEAD_DIM))
            outs['v'].append(v.reshape(batch, length, N_HEADS, HEAD_DIM))
            outs['f'].append(lf.reshape(batch, length, N_HEADS))
            outs['p'].append(sp)
        x = _ffn(x, W['norm_ffn2'], W['w_ffn2_in'], W['w_ffn2_out'], i,
                 g_final=W['norm_final'] if i == depth - 1 else None)
    return (x.reshape(batch, length, d),) + tuple(jnp.stack(outs[n]) for n in 'abkvfp')


def kernel(x_prompt, x_sample, state_conv_a, state_conv_b, cache_k, cache_v, cache_logf, state_pool, norm_ffn1, w_ffn1_in, w_ffn1_out, norm_mix, w_in_even, conv_a_w, conv_b_w, conv_b_bias, ln_b_gain, ln_b_bias, w_out_even, w_in_odd, b_forget, pool_w, pool_scale, w_out_odd, norm_ffn2, w_ffn2_in, w_ffn2_out, norm_final):
    depth, d = norm_ffn1.shape
    n_odd = w_in_odd.shape[0]
    w_odd = jnp.concatenate([
        w_in_odd[:, :, :3 * W_C],
        jnp.pad(w_in_odd[:, :, 3 * W_C:3 * W_C + N_HEADS], ((0, 0), (0, 0), (0, LANES - N_HEADS))),
        w_in_odd[:, :, 3 * W_C + N_HEADS:],
    ], axis=-1).astype(BF16)
    pool_bd = jnp.zeros((n_odd, W_D, W_D), F32)
    for g in range(len(POOL_WINDOWS)):
        sl = slice(g * GROUP_DIM, (g + 1) * GROUP_DIM)
        pool_bd = pool_bd.at[:, sl, sl].set(pool_w[:, g])
    W = {
        'norm_ffn1': norm_ffn1.reshape(depth, 1, d), 'norm_ffn2': norm_ffn2.reshape(depth, 1, d),
        'norm_mix': norm_mix.reshape(depth, 1, d), 'norm_final': norm_final.reshape(1, d),
        'w_ffn1_in': w_ffn1_in.astype(BF16), 'w_ffn1_out': w_ffn1_out.astype(BF16),
        'w_ffn2_in': w_ffn2_in.astype(BF16), 'w_ffn2_out': w_ffn2_out.astype(BF16),
        'w_in_even': w_in_even.astype(BF16), 'w_out_even': w_out_even.astype(BF16),
        'conv_a_w': conv_a_w, 'conv_b_w': conv_b_w,
        'conv_b_bias': conv_b_bias[:, None, :], 'ln_b_gain': ln_b_gain[:, None, :], 'ln_b_bias': ln_b_bias[:, None, :],
        'w_in_odd': w_odd, 'w_out_odd': w_out_odd.astype(BF16),
        'b_forget': jnp.pad(b_forget, ((0, 0), (0, LANES - N_HEADS)))[:, None, :],
        'pool_w': pool_bd.astype(BF16), 'pool_scale': pool_scale[:, None, :],
    }
    b_p = x_prompt.shape[0]
    zeros = lambda n, r, c: jnp.zeros((n, b_p, r, c), x_prompt.dtype)
    n_even = w_in_even.shape[0]
    prompt = _trunk(x_prompt, zeros(n_even, K_A - 1, W_A), zeros(n_even, K_B - 1, W_B), None,
                    zeros(n_odd, POOL_MAX - 1, W_D), W)
    sample = _trunk(x_sample, state_conv_a, state_conv_b, (cache_k, cache_v, cache_logf), state_pool, W)
    return (prompt[0], sample[0]) + prompt[1:] + sample[1:]
```

```python
import functools

import numpy as np
import jax
import jax.numpy as jnp
from jax import lax
from jax.experimental import pallas as pl
from jax.experimental.pallas import tpu as pltpu

F32 = jnp.float32
BF16 = jnp.bfloat16

EPS = 1e-6
NEG_INF = -1e30
GROUP_DIM = 64
HEAD_DIM = 64
N_HEADS = 12
W_A = 512
W_B = 512
K_A = 3
K_B = 31
W_C = N_HEADS * HEAD_DIM
W_D = 256
POOL_WINDOWS = (2, 4, 8, 16)
POOL_MAX = 16
LANES = 128
SUBLANES = 8
BF16_SUBLANES = 16
V_ROWS = HEAD_DIM + BF16_SUBLANES
HALO_A = 8
HALO_B = 32
HALO_P = 16
VMEM_LIMIT = 56 * 1024 * 1024

ROW_TILE = 512
ATT_TILE = 512
FF_CHUNK = 256
SCAN_BLOCK = 128


def _rms(x, g):
    return (x * lax.rsqrt(jnp.mean(x * x, axis=-1, keepdims=True) + EPS)) * g


def _dot(a, b):
    return jnp.dot(a, b, preferred_element_type=F32)


def _dot_nt(a, b):
    return lax.dot_general(a, b, (((1,), (1,)), ((), ())), preferred_element_type=F32)


def _dot_exact(a, b):
    return lax.dot_general(a, b, (((1,), (0,)), ((), ())), precision=lax.Precision.HIGHEST,
                           preferred_element_type=F32)


def _params(semantics):
    return pltpu.CompilerParams(dimension_semantics=semantics, vmem_limit_bytes=VMEM_LIMIT)


def _ffn_kernel(*refs, n_chunks, final):
    if final:
        x_ref, g_ref, wg_ref, wu_ref, wo_ref, gf_ref, o_ref, h_scr, acc_scr = refs
    else:
        x_ref, g_ref, wg_ref, wu_ref, wo_ref, o_ref, h_scr, acc_scr = refs
    h_scr[...] = _rms(x_ref[...], g_ref[...]).astype(BF16)
    for c in range(n_chunks):
        sl = slice(c * FF_CHUNK, (c + 1) * FF_CHUNK)
        hb = h_scr[...]
        gate = _dot(hb, wg_ref[:, sl])
        up = _dot(hb, wu_ref[:, sl])
        act = ((gate * jax.nn.sigmoid(gate)) * up).astype(BF16)
        part = _dot(act, wo_ref[sl, :])
        if c == 0:
            acc_scr[...] = part
        else:
            acc_scr[...] += part
    y = x_ref[...] + 0.5 * acc_scr[...]
    if final:
        y = _rms(y, gf_ref[...])
    o_ref[...] = y


def _ffn(x, g, w_in, w_out, layer, g_final=None):
    rows, d = x.shape
    d_ff = w_out.shape[1]
    tm = min(ROW_TILE, rows)
    final = g_final is not None
    in_specs = [
        pl.BlockSpec((tm, d), lambda i: (i, 0)),
        pl.BlockSpec((None, 1, d), lambda i: (layer, 0, 0)),
        pl.BlockSpec((None, d, d_ff), lambda i: (layer, 0, 0)),
        pl.BlockSpec((None, d, d_ff), lambda i: (layer, 0, 1)),
        pl.BlockSpec((None, d_ff, d), lambda i: (layer, 0, 0)),
    ]
    args = [x, g, w_in, w_in, w_out]
    if final:
        in_specs.append(pl.BlockSpec((1, d), lambda i: (0, 0)))
        args.append(g_final)
    return pl.pallas_call(
        functools.partial(_ffn_kernel, n_chunks=d_ff // FF_CHUNK, final=final),
        grid=(rows // tm,),
        in_specs=in_specs,
        out_specs=pl.BlockSpec((tm, d), lambda i: (i, 0)),
        out_shape=jax.ShapeDtypeStruct((rows, d), F32),
        scratch_shapes=[pltpu.VMEM((tm, d), BF16), pltpu.VMEM((tm, d), F32)],
        compiler_params=_params(("arbitrary",)),
        name="ffn",
    )(*args)


def _even_kernel(x_ref, g_ref, win_ref, wa_ref, wb_ref, bb_ref, lng_ref, lnb_ref, wout_ref, pa_ref, pb_ref,
                 o_ref, na_ref, nb_ref, ext_a, ext_b, shift_scr, *, nb, lt, n_l):
    j = pl.program_id(1)

    @pl.when(j == 0)
    def _():
        ext_a[:, HALO_A - (K_A - 1):HALO_A, :] = pa_ref[...]
        ext_b[:, HALO_B - (K_B - 1):HALO_B, :] = pb_ref[...]

    if n_l > 1:
        @pl.when(j > 0)
        def _():
            ext_a[:, 0:HALO_A, :] = ext_a[:, lt:lt + HALO_A, :]
            ext_b[:, 0:HALO_B, :] = ext_b[:, lt:lt + HALO_B, :]

    x = x_ref[...]
    h = _rms(x, g_ref[...]).astype(BF16)

    def proj(i, width):
        return _dot(h, win_ref[:, i * width:(i + 1) * width])

    ext_a[:, HALO_A:HALO_A + lt, :] = (proj(1, W_A) * proj(2, W_A)).reshape(nb, lt, W_A)
    ext_b[:, HALO_B:HALO_B + lt, :] = (proj(3, W_B) * jax.nn.sigmoid(proj(4, W_B))).reshape(nb, lt, W_B)

    def conv(ext, w_ref, taps, halo):
        base = halo - (taps - 1)
        acc = None
        for r in range(SUBLANES):
            offs = [base + k for k in range(taps) if (base + k) % SUBLANES == r]
            if not offs:
                continue
            if len(offs) == 1 or r == 0:
                windows = [ext[:, off:off + lt, :] for off in offs]
            else:
                buf = shift_scr.at[r % 2]
                span = offs[-1] - offs[0] + lt
                buf[:, 0:span, :] = ext[:, offs[0]:offs[0] + span, :]
                windows = [buf[:, off - offs[0]:off - offs[0] + lt, :] for off in offs]
            for off, win in zip(offs, windows):
                term = w_ref[off - base:off - base + 1, :] * win
                acc = term if acc is None else acc + term
        return acc.reshape(nb * lt, acc.shape[-1])

    ya = proj(0, W_A) * conv(ext_a, wa_ref, K_A, HALO_A)
    yc = conv(ext_b, wb_ref, K_B, HALO_B) + bb_ref[...]
    mu = jnp.mean(yc, axis=-1, keepdims=True)
    dev = yc - mu
    var = jnp.mean(dev * dev, axis=-1, keepdims=True)
    yn = dev * lax.rsqrt(var + EPS) * lng_ref[...] + lnb_ref[...]
    yb = yn * jax.nn.sigmoid(yn)
    y = _dot(ya.astype(BF16), wout_ref[0:W_A, :]) + _dot(yb.astype(BF16), wout_ref[W_A:W_A + W_B, :])
    o_ref[...] = x + y

    @pl.when(j == n_l - 1)
    def _():
        na_ref[...] = ext_a[:, HALO_A + lt - (K_A - 1):HALO_A + lt, :]
        nb_ref[...] = ext_b[:, HALO_B + lt - (K_B - 1):HALO_B + lt, :]


def _seq_tiling(batch, length):
    if length >= ROW_TILE:
        return 1, ROW_TILE, length // ROW_TILE
    nb = max(1, min(batch, ROW_TILE // length))
    return nb, length, 1


def _even(x, batch, length, g, w_in, conv_a, conv_b, bias_b, ln_g, ln_b, w_out, prev_a, prev_b, layer, j):
    rows, d = x.shape
    nb, lt, n_l = _seq_tiling(batch, length)
    tm = nb * lt
    d_in = w_in.shape[-1]
    row_map = lambda b, t: (b * n_l + t, 0)
    lay3 = lambda b, t: (j, 0, 0)
    seq_map = lambda b, t: (b, 0, 0)
    return pl.pallas_call(
        functools.partial(_even_kernel, nb=nb, lt=lt, n_l=n_l),
        grid=(batch // nb, n_l),
        in_specs=[
            pl.BlockSpec((tm, d), row_map),
            pl.BlockSpec((None, 1, d), lambda b, t: (layer, 0, 0)),
            pl.BlockSpec((None, d, d_in), lay3),
            pl.BlockSpec((None, K_A, W_A), lay3),
            pl.BlockSpec((None, K_B, W_B), lay3),
            pl.BlockSpec((None, 1, W_B), lay3),
            pl.BlockSpec((None, 1, W_B), lay3),
            pl.BlockSpec((None, 1, W_B), lay3),
            pl.BlockSpec((None, W_A + W_B, d), lay3),
            pl.BlockSpec((nb, K_A - 1, W_A), seq_map),
            pl.BlockSpec((nb, K_B - 1, W_B), seq_map),
        ],
        out_specs=[
            pl.BlockSpec((tm, d), row_map),
            pl.BlockSpec((nb, K_A - 1, W_A), seq_map),
            pl.BlockSpec((nb, K_B - 1, W_B), seq_map),
        ],
        out_shape=[
            jax.ShapeDtypeStruct((rows, d), F32),
            jax.ShapeDtypeStruct((batch, K_A - 1, W_A), F32),
            jax.ShapeDtypeStruct((batch, K_B - 1, W_B), F32),
        ],
        scratch_shapes=[pltpu.VMEM((nb, HALO_A + lt, W_A), F32), pltpu.VMEM((nb, HALO_B + lt, W_B), F32),
                        pltpu.VMEM((2, nb, HALO_B + lt, W_B), F32)],
        compiler_params=_params(("arbitrary", "arbitrary")),
        name="even",
    )(x, g, w_in, conv_a, conv_b, bias_b, ln_g, ln_b, w_out, prev_a, prev_b)


def _split3(f):
    hi = f.astype(BF16)
    r1 = f - hi.astype(F32)
    mid = r1.astype(BF16)
    lo = (r1 - mid.astype(F32)).astype(BF16)
    return hi, mid, lo


def _odd_in_kernel(*refs, lt, prompt):
    x_ref, g_ref, w_ref, bf_ref = refs[:4]
    if prompt:
        pq_ref, pk_ref, oq_ref, ok_ref = refs[4:8]
        qa_ref, ka_ref, vt_ref, k_ref, v_ref, lf_ref, u_ref, carry = refs[8:]
    else:
        q_ref, kb_ref, vb_ref, k_ref, v_ref, lf_ref, lfp_ref, u_ref = refs[4:]
    h = _rms(x_ref[...], g_ref[...]).astype(BF16)
    q = (_dot(h, w_ref[:, 0:W_C]) * (HEAD_DIM ** -0.5)).astype(BF16)
    k = _dot(h, w_ref[:, W_C:2 * W_C])
    k_ref[...] = k
    kb = k.astype(BF16)
    v = _dot(h, w_ref[:, 2 * W_C:3 * W_C])
    v_ref[...] = v
    z = _dot(h, w_ref[:, 3 * W_C:3 * W_C + LANES]) + bf_ref[...]
    lf = jnp.minimum(z, 0.0) - jnp.log1p(jnp.exp(-jnp.abs(z)))
    lf_ref[...] = lf[:, :N_HEADS]
    u_ref[...] = _dot(h, w_ref[:, 3 * W_C + LANES:3 * W_C + LANES + W_D])
    if not prompt:
        q_ref[...] = q
        kb_ref[...] = kb
        vb_ref[...] = v.astype(BF16)
        lfp_ref[...] = lf
        return

    @pl.when(pl.program_id(0) == 0)
    def _():
        carry[...] = jnp.zeros_like(carry)

    r = lax.broadcasted_iota(jnp.int32, (lt, lt), 0)
    c = lax.broadcasted_iota(jnp.int32, (lt, lt), 1)
    f = _dot_exact((c <= r).astype(F32), lf) + carry[...]
    carry[...] = f[lt - 1:lt, :]
    fa = jnp.concatenate(_split3(f), axis=1)
    aug_q = (_dot(fa, pq_ref[...]) + oq_ref[...]).astype(BF16)
    aug_k = (_dot(fa, pk_ref[...]) + ok_ref[...]).astype(BF16)
    vt = v.T.astype(BF16)
    upper_half = lax.broadcasted_iota(jnp.int32, (1, LANES), 1) >= HEAD_DIM
    ones = jnp.ones((BF16_SUBLANES, lt), BF16)
    for p in range(N_HEADS // 2):
        sl = slice(p * LANES, (p + 1) * LANES)
        ka_ref[p, :, 0:LANES] = kb[:, sl]
        ka_ref[p, :, LANES:2 * LANES] = aug_k[:, sl]
        for hh in range(2):
            h = 2 * p + hh
            keep = upper_half == (hh == 1)
            qa_ref[h, :, 0:LANES] = jnp.where(keep, q[:, sl], jnp.zeros_like(q[:, sl]))
            qa_ref[h, :, LANES:2 * LANES] = jnp.where(keep, aug_q[:, sl], jnp.zeros_like(aug_q[:, sl]))
            vt_ref[h, 0:HEAD_DIM, :] = vt[h * HEAD_DIM:(h + 1) * HEAD_DIM, :]
            vt_ref[h, HEAD_DIM:V_ROWS, :] = ones


def _placement():
    pq = np.zeros((3 * LANES, W_C), np.float32)
    pk = np.zeros((3 * LANES, W_C), np.float32)
    oq = np.zeros((1, W_C), np.float32)
    ok = np.zeros((1, W_C), np.float32)
    for h in range(N_HEADS):
        base = (h // 2) * LANES + (h % 2) * HEAD_DIM
        for part in range(3):
            pq[part * LANES + h, base + part] = 1.0
            pk[part * LANES + h, base + 3 + part] = -1.0
            ok[0, base + part] = 1.0
            oq[0, base + 3 + part] = 1.0
    return jnp.asarray(pq, BF16), jnp.asarray(pk, BF16), jnp.asarray(oq), jnp.asarray(ok)


def _odd_in(x, g, w, b_f, layer, j, prompt):
    rows, d = x.shape
    tm = min(ROW_TILE, rows)
    n = rows // tm
    d_in = w.shape[-1]
    n_pairs = N_HEADS // 2
    row = lambda i: (i, 0)
    lay3 = lambda i: (j, 0, 0)
    const = lambda i: (0, 0)
    in_specs = [
        pl.BlockSpec((tm, d), row),
        pl.BlockSpec((None, 1, d), lambda i: (layer, 0, 0)),
        pl.BlockSpec((None, d, d_in), lay3),
        pl.BlockSpec((None, 1, LANES), lay3),
    ]
    args = [x, g, w, b_f]
    common_specs = [pl.BlockSpec((tm, W_C), row), pl.BlockSpec((tm, W_C), row), pl.BlockSpec((tm, N_HEADS), row)]
    common_shape = [jax.ShapeDtypeStruct((rows, W_C), F32), jax.ShapeDtypeStruct((rows, W_C), F32),
                    jax.ShapeDtypeStruct((rows, N_HEADS), F32)]
    u_spec, u_shape = pl.BlockSpec((tm, W_D), row), jax.ShapeDtypeStruct((rows, W_D), F32)
    if prompt:
        in_specs += [pl.BlockSpec((3 * LANES, W_C), const), pl.BlockSpec((3 * LANES, W_C), const),
                     pl.BlockSpec((1, W_C), const), pl.BlockSpec((1, W_C), const)]
        args += list(_placement())
        out_specs = [pl.BlockSpec((N_HEADS, tm, 2 * LANES), lambda i: (0, i, 0)),
                     pl.BlockSpec((n_pairs, tm, 2 * LANES), lambda i: (0, i, 0)),
                     pl.BlockSpec((N_HEADS, V_ROWS, tm), lambda i: (0, 0, i))] + common_specs + [u_spec]
        out_shape = [jax.ShapeDtypeStruct((N_HEADS, rows, 2 * LANES), BF16),
                     jax.ShapeDtypeStruct((n_pairs, rows, 2 * LANES), BF16),
                     jax.ShapeDtypeStruct((N_HEADS, V_ROWS, rows), BF16)] + common_shape + [u_shape]
        scratch = [pltpu.VMEM((1, LANES), F32)]
    else:
        out_specs = [pl.BlockSpec((tm, W_C), row)] * 3 + common_specs + [pl.BlockSpec((tm, LANES), row), u_spec]
        out_shape = [jax.ShapeDtypeStruct((rows, W_C), BF16)] * 3 + common_shape + [
            jax.ShapeDtypeStruct((rows, LANES), F32), u_shape]
        scratch = []
    return pl.pallas_call(
        functools.partial(_odd_in_kernel, lt=tm, prompt=prompt),
        grid=(n,),
        in_specs=in_specs,
        out_specs=out_specs,
        out_shape=out_shape,
        scratch_shapes=scratch,
        compiler_params=_params(("arbitrary",)),
        name="odd_in",
    )(*args)


def _flash_kernel(qi_tab, ki_tab, qa_ref, ka_ref, vt_ref, o_ref, m_scr, acc_scr, *, tq):
    t = pl.program_id(0)
    qi = qi_tab[t]
    ki = ki_tab[t]

    @pl.when(ki == 0)
    def _():
        m_scr[...] = jnp.full_like(m_scr, NEG_INF)
        acc_scr[...] = jnp.zeros_like(acc_scr)

    def pair_scores(p):
        ka = ka_ref[p]
        return [_dot_nt(ka, qa_ref[h]) for h in (2 * p, 2 * p + 1)]

    def pair_update(p, scores, masked):
        heads = (2 * p, 2 * p + 1)
        m_prev = [m_scr[h] for h in heads]
        acc_prev = [acc_scr[h] for h in heads]
        m_out, acc_out = [], []
        for hh, h in enumerate(heads):
            st = scores[hh]
            if masked:
                key = lax.broadcasted_iota(jnp.int32, st.shape, 0)
                qry = lax.broadcasted_iota(jnp.int32, st.shape, 1)
                st = jnp.where(key <= qry, st, NEG_INF)
            m_new = jnp.maximum(m_prev[hh], jnp.max(st, axis=0, keepdims=True))
            alpha = jnp.exp(m_prev[hh] - m_new)
            e = jnp.exp(st - m_new).astype(BF16)
            acc_out.append(alpha * acc_prev[hh] + _dot(vt_ref[h], e))
            m_out.append(m_new)
        for hh, h in enumerate(heads):
            m_scr[h] = m_out[hh]
            acc_scr[h] = acc_out[hh]

    @pl.when(ki < qi)
    def _():
        scores = pair_scores(0)
        for p in range(N_HEADS // 2):
            ahead = pair_scores(p + 1) if p + 1 < N_HEADS // 2 else None
            pair_update(p, scores, False)
            scores = ahead

    @pl.when(ki == qi)
    def _():
        def pair(p, carry):
            pair_update(p, pair_scores(p), True)
            return carry

        lax.fori_loop(0, N_HEADS // 2, pair, 0)
        for p in range(N_HEADS // 2):
            pair_out = [acc_scr[h, 0:HEAD_DIM, :] / acc_scr[h, HEAD_DIM:HEAD_DIM + 1, :] for h in (2 * p, 2 * p + 1)]
            o_ref[:, p * LANES:(p + 1) * LANES] = jnp.concatenate(pair_out, axis=0).T.astype(BF16)


def _flash(qa, ka, vt):
    rows = qa.shape[1]
    n_pairs = N_HEADS // 2
    tq = min(ATT_TILE, rows)
    nq = rows // tq
    qi_tab = np.concatenate([np.full(i + 1, i, np.int32) for i in range(nq)])
    ki_tab = np.concatenate([np.arange(i + 1, dtype=np.int32) for i in range(nq)])
    grid_spec = pltpu.PrefetchScalarGridSpec(
        num_scalar_prefetch=2,
        grid=(qi_tab.shape[0],),
        in_specs=[
            pl.BlockSpec((N_HEADS, tq, 2 * LANES), lambda t, qt, kt: (0, qt[t], 0)),
            pl.BlockSpec((n_pairs, tq, 2 * LANES), lambda t, qt, kt: (0, kt[t], 0)),
            pl.BlockSpec((N_HEADS, V_ROWS, tq), lambda t, qt, kt: (0, 0, kt[t])),
        ],
        out_specs=pl.BlockSpec((tq, W_C), lambda t, qt, kt: (qt[t], 0)),
        scratch_shapes=[pltpu.VMEM((N_HEADS, 1, tq), F32), pltpu.VMEM((N_HEADS, V_ROWS, tq), F32)],
    )
    return pl.pallas_call(
        functools.partial(_flash_kernel, tq=tq),
        grid_spec=grid_spec,
        out_shape=jax.ShapeDtypeStruct((rows, W_C), BF16),
        compiler_params=_params(("arbitrary",)),
        name="flash",
    )(jnp.asarray(qi_tab), jnp.asarray(ki_tab), qa, ka, vt)


def _decode_kernel(q_ref, kn_ref, vn_ref, ck_ref, cv_ref, cl_ref, lfp_ref, o_ref, kall, vall, lfa, gcol, grow,
                   *, past, length, s_pad):
    p = pl.program_id(1)
    lane = lax.broadcasted_iota(jnp.int32, (1, LANES), 1)
    n_blk = s_pad // SCAN_BLOCK

    @pl.when(p == 0)
    def _():
        lfa[...] = jnp.zeros_like(lfa)
        lfa[0:past, 0:N_HEADS] = cl_ref[...]
        lfa[past:past + length, :] = lfp_ref[...]
        r = lax.broadcasted_iota(jnp.int32, (SCAN_BLOCK, SCAN_BLOCK), 0)
        c = lax.broadcasted_iota(jnp.int32, (SCAN_BLOCK, SCAN_BLOCK), 1)
        upper = (c > r).astype(F32)
        carry = jnp.zeros((1, LANES), F32)
        for b in reversed(range(n_blk)):
            blk = lfa[b * SCAN_BLOCK:(b + 1) * SCAN_BLOCK, :]
            g = _dot_exact(upper, blk) + carry
            carry = carry + jnp.sum(blk, axis=0, keepdims=True)
            gcol[b * SCAN_BLOCK:(b + 1) * SCAN_BLOCK, :] = g
            grow[:, b * SCAN_BLOCK:(b + 1) * SCAN_BLOCK] = g.T
        kall[past + length:s_pad, :] = jnp.zeros((s_pad - past - length, LANES), BF16)
        vall[past + length:s_pad, :] = jnp.zeros((s_pad - past - length, LANES), BF16)

    kall[0:past, :] = ck_ref[...].astype(BF16)
    vall[0:past, :] = cv_ref[...].astype(BF16)
    kall[past:past + length, :] = kn_ref[...]
    vall[past:past + length, :] = vn_ref[...]
    q = q_ref[...]
    row = lax.broadcasted_iota(jnp.int32, (length, s_pad), 0)
    col = lax.broadcasted_iota(jnp.int32, (length, s_pad), 1)
    visible = col <= row + past
    outs = []
    for hh in range(2):
        qm = jnp.where((lane >= HEAD_DIM) == (hh == 1), q, jnp.zeros_like(q))
        s = _dot_nt(qm, kall[...])
        gq = jnp.sum(jnp.where(lane == 2 * p + hh, gcol[past:past + length, :], 0.0), axis=1, keepdims=True)
        gk = grow[pl.ds(2 * p + hh, 1), :]
        s = jnp.where(visible, s + (gk - gq), NEG_INF)
        m = jnp.max(s, axis=1, keepdims=True)
        e = jnp.exp(s - m)
        l = jnp.sum(e, axis=1, keepdims=True)
        outs.append(_dot(e.astype(BF16), vall[...]) / l)
    o_ref[...] = jnp.where(lane < HEAD_DIM, outs[0], outs[1]).astype(BF16)


def _decode(q, kn, vn, cache_k, cache_v, cache_logf, lfp, batch, length, j):
    rows = q.shape[0]
    past = cache_k.shape[2]
    s_pad = past + LANES
    ck = cache_k.reshape(cache_k.shape[0] * batch, past, W_C)
    cv = cache_v.reshape(cache_v.shape[0] * batch, past, W_C)
    cl = cache_logf.reshape(cache_logf.shape[0] * batch, past, N_HEADS)
    new_map = lambda b, p: (b, p)
    cache_map = lambda b, p: (j * batch + b, 0, p)
    return pl.pallas_call(
        functools.partial(_decode_kernel, past=past, length=length, s_pad=s_pad),
        grid=(batch, N_HEADS // 2),
        in_specs=[
            pl.BlockSpec((length, LANES), new_map),
            pl.BlockSpec((length, LANES), new_map),
            pl.BlockSpec((length, LANES), new_map),
            pl.BlockSpec((None, past, LANES), cache_map),
            pl.BlockSpec((None, past, LANES), cache_map),
            pl.BlockSpec((None, past, N_HEADS), lambda b, p: (j * batch + b, 0, 0)),
            pl.BlockSpec((length, LANES), lambda b, p: (b, 0)),
        ],
        out_specs=pl.BlockSpec((length, LANES), new_map),
        out_shape=jax.ShapeDtypeStruct((rows, W_C), BF16),
        scratch_shapes=[
            pltpu.VMEM((s_pad, LANES), BF16), pltpu.VMEM((s_pad, LANES), BF16),
            pltpu.VMEM((s_pad, LANES), F32), pltpu.VMEM((s_pad, LANES), F32), pltpu.VMEM((LANES, s_pad), F32),
        ],
        compiler_params=_params(("arbitrary", "arbitrary")),
        name="decode",
    )(q, kn, vn, ck, cv, cl, lfp)


def _odd_out_kernel(o_ref, u_ref, x_ref, pp_ref, pw_ref, ps_ref, wout_ref, y_ref, np_ref, ext, *, nb, lt, n_l, pos0):
    j = pl.program_id(1)
    hist = POOL_MAX - 1

    @pl.when(j == 0)
    def _():
        ext[:, HALO_P - hist:HALO_P, :] = pp_ref[...]

    if n_l > 1:
        @pl.when(j > 0)
        def _():
            ext[:, 0:HALO_P, :] = ext[:, lt:lt + HALO_P, :]

    u = u_ref[...].reshape(nb, lt, W_D)
    ext[:, HALO_P:HALO_P + lt, :] = u
    group = lax.broadcasted_iota(jnp.int32, (nb, lt, W_D), 2) // GROUP_DIM
    acc = u
    win = None
    for back in range(1, POOL_MAX):
        acc = acc + ext[:, HALO_P - back:HALO_P - back + lt, :]
        if back + 1 in POOL_WINDOWS:
            g = POOL_WINDOWS.index(back + 1)
            win = acc if win is None else jnp.where(group >= g, acc, win)
    width = jnp.left_shift(2, group)
    pos = pos0 + j * lt + lax.broadcasted_iota(jnp.int32, (nb, lt, W_D), 1)
    cnt = jnp.minimum(width, pos + 1).astype(F32)
    z = (win / cnt - u).reshape(nb * lt, W_D)
    yd = _dot(z.astype(BF16), pw_ref[...]) * ps_ref[...]
    y = _dot(o_ref[...], wout_ref[0:W_C, :]) + _dot(yd.astype(BF16), wout_ref[W_C:W_C + W_D, :])
    y_ref[...] = x_ref[...] + y

    @pl.when(j == n_l - 1)
    def _():
        np_ref[...] = ext[:, HALO_P + lt - hist:HALO_P + lt, :]


def _odd_out(o, u, x, batch, length, pos0, prev_pool, pool_w, pool_scale, w_out, j):
    rows, d = x.shape
    nb, lt, n_l = _seq_tiling(batch, length)
    tm = nb * lt
    row_map = lambda b, t: (b * n_l + t, 0)
    lay3 = lambda b, t: (j, 0, 0)
    seq_map = lambda b, t: (b, 0, 0)
    return pl.pallas_call(
        functools.partial(_odd_out_kernel, nb=nb, lt=lt, n_l=n_l, pos0=pos0),
        grid=(batch // nb, n_l),
        in_specs=[
            pl.BlockSpec((tm, W_C), row_map),
            pl.BlockSpec((tm, W_D), row_map),
            pl.BlockSpec((tm, d), row_map),
            pl.BlockSpec((nb, POOL_MAX - 1, W_D), seq_map),
            pl.BlockSpec((None, W_D, W_D), lay3),
            pl.BlockSpec((None, 1, W_D), lay3),
            pl.BlockSpec((None, W_C + W_D, d), lay3),
        ],
        out_specs=[pl.BlockSpec((tm, d), row_map), pl.BlockSpec((nb, POOL_MAX - 1, W_D), seq_map)],
        out_shape=[jax.ShapeDtypeStruct((rows, d), F32), jax.ShapeDtypeStruct((batch, POOL_MAX - 1, W_D), F32)],
        scratch_shapes=[pltpu.VMEM((nb, HALO_P + lt, W_D), F32)],
        compiler_params=_params(("arbitrary", "arbitrary")),
        name="odd_out",
    )(o, u, x, prev_pool, pool_w, pool_scale, w_out)


def _trunk(x3, prev_a, prev_b, cache, prev_pool, W):
    batch, length, d = x3.shape
    depth = W['w_ffn1_in'].shape[0]
    x = x3.reshape(batch * length, d)
    pos0 = 0 if cache is None else cache[0].shape[2]
    outs = {n: [] for n in 'abkvfp'}
    for i in range(depth):
        j = i // 2
        x = _ffn(x, W['norm_ffn1'], W['w_ffn1_in'], W['w_ffn1_out'], i)
        if i % 2 == 0:
            x, sa, sb = _even(x, batch, length, W['norm_mix'], W['w_in_even'], W['conv_a_w'], W['conv_b_w'],
                              W['conv_b_bias'], W['ln_b_gain'], W['ln_b_bias'], W['w_out_even'],
                              prev_a[j], prev_b[j], i, j)
            outs['a'].append(sa)
            outs['b'].append(sb)
        else:
            res = _odd_in(x, W['norm_mix'], W['w_in_odd'], W['b_forget'], i, j, prompt=cache is None)
            if cache is None:
                qa, ka, vt, k, v, lf, u = res
                o = _flash(qa, ka, vt)
            else:
                q, kb, vb, k, v, lf, lfp, u = res
                o = _decode(q, kb, vb, cache[0], cache[1], cache[2], lfp, batch, length, j)
            x, sp = _odd_out(o, u, x, batch, length, pos0, prev_pool[j], W['pool_w'], W['pool_scale'],
                             W['w_out_odd'], j)
            outs['k'].append(k.reshape(batch, length, N_HEADS, HEAD_DIM))
            outs['v'].append(v.reshape(batch, length, N_HEADS, HEAD_DIM))
            outs['f'].append(lf.reshape(batch, length, N_HEADS))
            outs['p'].append(sp)
        x = _ffn(x, W['norm_ffn2'], W['w_ffn2_in'], W['w_ffn2_out'], i,
                 g_final=W['norm_final'] if i == depth - 1 else None)
    return (x.reshape(batch, length, d),) + tuple(jnp.stack(outs[n]) for n in 'abkvfp')


def kernel(x_prompt, x_sample, state_conv_a, state_conv_b, cache_k, cache_v, cache_logf, state_pool, norm_ffn1, w_ffn1_in, w_ffn1_out, norm_mix, w_in_even, conv_a_w, conv_b_w, conv_b_bias, ln_b_gain, ln_b_bias, w_out_even, w_in_odd, b_forget, pool_w, pool_scale, w_out_odd, norm_ffn2, w_ffn2_in, w_ffn2_out, norm_final):
    depth, d = norm_ffn1.shape
    n_odd = w_in_odd.shape[0]
    w_odd = jnp.concatenate([
        w_in_odd[:, :, :3 * W_C],
        jnp.pad(w_in_odd[:, :, 3 * W_C:3 * W_C + N_HEADS], ((0, 0), (0, 0), (0, LANES - N_HEADS))),
        w_in_odd[:, :, 3 * W_C + N_HEADS:],
    ], axis=-1).astype(BF16)
    pool_bd = jnp.zeros((n_odd, W_D, W_D), F32)
    for g in range(len(POOL_WINDOWS)):
        sl = slice(g * GROUP_DIM, (g + 1) * GROUP_DIM)
        pool_bd = pool_bd.at[:, sl, sl].set(pool_w[:, g])
    W = {
        'norm_ffn1': norm_ffn1.reshape(depth, 1, d), 'norm_ffn2': norm_ffn2.reshape(depth, 1, d),
        'norm_mix': norm_mix.reshape(depth, 1, d), 'norm_final': norm_final.reshape(1, d),
        'w_ffn1_in': w_ffn1_in.astype(BF16), 'w_ffn1_out': w_ffn1_out.astype(BF16),
        'w_ffn2_in': w_ffn2_in.astype(BF16), 'w_ffn2_out': w_ffn2_out.astype(BF16),
        'w_in_even': w_in_even.astype(BF16), 'w_out_even': w_out_even.astype(BF16),
        'conv_a_w': conv_a_w, 'conv_b_w': conv_b_w,
        'conv_b_bias': conv_b_bias[:, None, :], 'ln_b_gain': ln_b_gain[:, None, :], 'ln_b_bias': ln_b_bias[:, None, :],
        'w_in_odd': w_odd, 'w_out_odd': w_out_odd.astype(BF16),
        'b_forget': jnp.pad(b_forget, ((0, 0), (0, LANES - N_HEADS)))[:, None, :],
        'pool_w': pool_bd.astype(BF16), 'pool_scale': pool_scale[:, None, :],
    }
    b_p = x_prompt.shape[0]
    zeros = lambda n, r, c: jnp.zeros((n, b_p, r, c), x_prompt.dtype)
    n_even = w_in_even.shape[0]
    prompt = _trunk(x_prompt, zeros(n_even, K_A - 1, W_A), zeros(n_even, K_B - 1, W_B), None,
                    zeros(n_odd, POOL_MAX - 1, W_D), W)
    sample = _trunk(x_sample, state_conv_a, state_conv_b, (cache_k, cache_v, cache_logf), state_pool, W)
    return (prompt[0], sample[0]) + prompt[1:] + sample[1:]
```

```python
import functools

import numpy as np
import jax
import jax.numpy as jnp
from jax import lax
from jax.experimental import pallas as pl
from jax.experimental.pallas import tpu as pltpu

F32 = jnp.float32
BF16 = jnp.bfloat16

EPS = 1e-6
NEG_INF = -1e30
GROUP_DIM = 64
HEAD_DIM = 64
N_HEADS = 12
W_A = 512
W_B = 512
K_A = 3
K_B = 31
W_C = N_HEADS * HEAD_DIM
W_D = 256
POOL_WINDOWS = (2, 4, 8, 16)
POOL_MAX = 16
LANES = 128
SUBLANES = 8
BF16_SUBLANES = 16
V_ROWS = HEAD_DIM + BF16_SUBLANES
HALO_A = 8
HALO_B = 32
HALO_P = 16
VMEM_LIMIT = 56 * 1024 * 1024

ROW_TILE = 512
ATT_KEY_TILE = 512
ATT_QUERY_TILE = 512
FF_CHUNK = 256
SCAN_BLOCK = 128
FLASH_GROUP = 2
LOG2E = 1.4426950408889634


def _rms(x, g):
    return (x * lax.rsqrt(jnp.mean(x * x, axis=-1, keepdims=True) + EPS)) * g


def _dot(a, b):
    return jnp.dot(a, b, preferred_element_type=F32)


def _dot_nt(a, b):
    return lax.dot_general(a, b, (((1,), (1,)), ((), ())), preferred_element_type=F32)


def _dot_exact(a, b):
    return lax.dot_general(a, b, (((1,), (0,)), ((), ())), precision=lax.Precision.HIGHEST,
                           preferred_element_type=F32)


def _params(semantics):
    return pltpu.CompilerParams(dimension_semantics=semantics, vmem_limit_bytes=VMEM_LIMIT)


def _ffn_kernel(*refs, n_chunks, final):
    if final:
        x_ref, g_ref, wg_ref, wu_ref, wo_ref, gf_ref, o_ref, h_scr, acc_scr = refs
    else:
        x_ref, g_ref, wg_ref, wu_ref, wo_ref, o_ref, h_scr, acc_scr = refs
    h_scr[...] = _rms(x_ref[...], g_ref[...]).astype(BF16)
    for c in range(n_chunks):
        sl = slice(c * FF_CHUNK, (c + 1) * FF_CHUNK)
        hb = h_scr[...]
        gate = _dot(hb, wg_ref[:, sl])
        up = _dot(hb, wu_ref[:, sl])
        act = ((gate * jax.nn.sigmoid(gate)) * up).astype(BF16)
        part = _dot(act, wo_ref[sl, :])
        if c == 0:
            acc_scr[...] = part
        else:
            acc_scr[...] += part
    y = x_ref[...] + 0.5 * acc_scr[...]
    if final:
        y = _rms(y, gf_ref[...])
    o_ref[...] = y


def _ffn(x, g, w_in, w_out, layer, g_final=None):
    rows, d = x.shape
    d_ff = w_out.shape[1]
    tm = min(ROW_TILE, rows)
    final = g_final is not None
    in_specs = [
        pl.BlockSpec((tm, d), lambda i: (i, 0)),
        pl.BlockSpec((None, 1, d), lambda i: (layer, 0, 0)),
        pl.BlockSpec((None, d, d_ff), lambda i: (layer, 0, 0)),
        pl.BlockSpec((None, d, d_ff), lambda i: (layer, 0, 1)),
        pl.BlockSpec((None, d_ff, d), lambda i: (layer, 0, 0)),
    ]
    args = [x, g, w_in, w_in, w_out]
    if final:
        in_specs.append(pl.BlockSpec((1, d), lambda i: (0, 0)))
        args.append(g_final)
    return pl.pallas_call(
        functools.partial(_ffn_kernel, n_chunks=d_ff // FF_CHUNK, final=final),
        grid=(rows // tm,),
        in_specs=in_specs,
        out_specs=pl.BlockSpec((tm, d), lambda i: (i, 0)),
        out_shape=jax.ShapeDtypeStruct((rows, d), F32),
        scratch_shapes=[pltpu.VMEM((tm, d), BF16), pltpu.VMEM((tm, d), F32)],
        compiler_params=_params(("arbitrary",)),
        name="ffn",
    )(*args)


def _even_kernel(x_ref, g_ref, win_ref, wa_ref, wb_ref, bb_ref, lng_ref, lnb_ref, wout_ref, pa_ref, pb_ref,
                 o_ref, na_ref, nb_ref, ext_a, ext_b, shift_scr, *, nb, lt, n_l):
    j = pl.program_id(1)

    @pl.when(j == 0)
    def _():
        ext_a[:, HALO_A - (K_A - 1):HALO_A, :] = pa_ref[...]
        ext_b[:, HALO_B - (K_B - 1):HALO_B, :] = pb_ref[...]

    if n_l > 1:
        @pl.when(j > 0)
        def _():
            ext_a[:, 0:HALO_A, :] = ext_a[:, lt:lt + HALO_A, :]
            ext_b[:, 0:HALO_B, :] = ext_b[:, lt:lt + HALO_B, :]

    x = x_ref[...]
    h = _rms(x, g_ref[...]).astype(BF16)

    def proj(i, width):
        return _dot(h, win_ref[:, i * width:(i + 1) * width])

    ext_a[:, HALO_A:HALO_A + lt, :] = (proj(1, W_A) * proj(2, W_A)).reshape(nb, lt, W_A)
    ext_b[:, HALO_B:HALO_B + lt, :] = (proj(3, W_B) * jax.nn.sigmoid(proj(4, W_B))).reshape(nb, lt, W_B)

    def conv(ext, w_ref, taps, halo):
        base = halo - (taps - 1)
        acc = None
        for r in range(SUBLANES):
            offs = [base + k for k in range(taps) if (base + k) % SUBLANES == r]
            if not offs:
                continue
            if len(offs) == 1 or r == 0:
                windows = [ext[:, off:off + lt, :] for off in offs]
            else:
                buf = shift_scr.at[r % 2]
                span = offs[-1] - offs[0] + lt
                buf[:, 0:span, :] = ext[:, offs[0]:offs[0] + span, :]
                windows = [buf[:, off - offs[0]:off - offs[0] + lt, :] for off in offs]
            for off, win in zip(offs, windows):
                term = w_ref[off - base:off - base + 1, :] * win
                acc = term if acc is None else acc + term
        return acc.reshape(nb * lt, acc.shape[-1])

    ya = proj(0, W_A) * conv(ext_a, wa_ref, K_A, HALO_A)
    yc = conv(ext_b, wb_ref, K_B, HALO_B) + bb_ref[...]
    mu = jnp.mean(yc, axis=-1, keepdims=True)
    dev = yc - mu
    var = jnp.mean(dev * dev, axis=-1, keepdims=True)
    yn = dev * lax.rsqrt(var + EPS) * lng_ref[...] + lnb_ref[...]
    yb = yn * jax.nn.sigmoid(yn)
    y = _dot(ya.astype(BF16), wout_ref[0:W_A, :]) + _dot(yb.astype(BF16), wout_ref[W_A:W_A + W_B, :])
    o_ref[...] = x + y

    @pl.when(j == n_l - 1)
    def _():
        na_ref[...] = ext_a[:, HALO_A + lt - (K_A - 1):HALO_A + lt, :]
        nb_ref[...] = ext_b[:, HALO_B + lt - (K_B - 1):HALO_B + lt, :]


def _seq_tiling(batch, length):
    if length >= ROW_TILE:
        return 1, ROW_TILE, length // ROW_TILE
    nb = max(1, min(batch, ROW_TILE // length))
    return nb, length, 1


def _even(x, batch, length, g, w_in, conv_a, conv_b, bias_b, ln_g, ln_b, w_out, prev_a, prev_b, layer, j):
    rows, d = x.shape
    nb, lt, n_l = _seq_tiling(batch, length)
    tm = nb * lt
    d_in = w_in.shape[-1]
    row_map = lambda b, t: (b * n_l + t, 0)
    lay3 = lambda b, t: (j, 0, 0)
    seq_map = lambda b, t: (b, 0, 0)
    return pl.pallas_call(
        functools.partial(_even_kernel, nb=nb, lt=lt, n_l=n_l),
        grid=(batch // nb, n_l),
        in_specs=[
            pl.BlockSpec((tm, d), row_map),
            pl.BlockSpec((None, 1, d), lambda b, t: (layer, 0, 0)),
            pl.BlockSpec((None, d, d_in), lay3),
            pl.BlockSpec((None, K_A, W_A), lay3),
            pl.BlockSpec((None, K_B, W_B), lay3),
            pl.BlockSpec((None, 1, W_B), lay3),
            pl.BlockSpec((None, 1, W_B), lay3),
            pl.BlockSpec((None, 1, W_B), lay3),
            pl.BlockSpec((None, W_A + W_B, d), lay3),
            pl.BlockSpec((nb, K_A - 1, W_A), seq_map),
            pl.BlockSpec((nb, K_B - 1, W_B), seq_map),
        ],
        out_specs=[
            pl.BlockSpec((tm, d), row_map),
            pl.BlockSpec((nb, K_A - 1, W_A), seq_map),
            pl.BlockSpec((nb, K_B - 1, W_B), seq_map),
        ],
        out_shape=[
            jax.ShapeDtypeStruct((rows, d), F32),
            jax.ShapeDtypeStruct((batch, K_A - 1, W_A), F32),
            jax.ShapeDtypeStruct((batch, K_B - 1, W_B), F32),
        ],
        scratch_shapes=[pltpu.VMEM((nb, HALO_A + lt, W_A), F32), pltpu.VMEM((nb, HALO_B + lt, W_B), F32),
                        pltpu.VMEM((2, nb, HALO_B + lt, W_B), F32)],
        compiler_params=_params(("arbitrary", "arbitrary")),
        name="even",
    )(x, g, w_in, conv_a, conv_b, bias_b, ln_g, ln_b, w_out, prev_a, prev_b)


def _split3(f):
    hi = f.astype(BF16)
    r1 = f - hi.astype(F32)
    mid = r1.astype(BF16)
    lo = (r1 - mid.astype(F32)).astype(BF16)
    return hi, mid, lo


def _odd_in_kernel(*refs, lt, prompt):
    x_ref, g_ref, w_ref, bf_ref = refs[:4]
    if prompt:
        pq_ref, pk_ref, oq_ref, ok_ref = refs[4:8]
        qa_ref, ka_ref, vt_ref, k_ref, v_ref, lf_ref, u_ref, carry = refs[8:]
    else:
        q_ref, kb_ref, vb_ref, k_ref, v_ref, lf_ref, lfp_ref, u_ref = refs[4:]
    h = _rms(x_ref[...], g_ref[...]).astype(BF16)
    q_scale = HEAD_DIM ** -0.5 * (LOG2E if prompt else 1.0)
    q = (_dot(h, w_ref[:, 0:W_C]) * q_scale).astype(BF16)
    k = _dot(h, w_ref[:, W_C:2 * W_C])
    k_ref[...] = k
    kb = k.astype(BF16)
    v = _dot(h, w_ref[:, 2 * W_C:3 * W_C])
    v_ref[...] = v
    z = _dot(h, w_ref[:, 3 * W_C:3 * W_C + LANES]) + bf_ref[...]
    lf = jnp.minimum(z, 0.0) - jnp.log1p(jnp.exp(-jnp.abs(z)))
    lf_ref[...] = lf[:, :N_HEADS]
    u_ref[...] = _dot(h, w_ref[:, 3 * W_C + LANES:3 * W_C + LANES + W_D])
    if not prompt:
        q_ref[...] = q
        kb_ref[...] = kb
        vb_ref[...] = v.astype(BF16)
        lfp_ref[...] = lf
        return

    @pl.when(pl.program_id(0) == 0)
    def _():
        carry[...] = jnp.zeros_like(carry)

    r = lax.broadcasted_iota(jnp.int32, (lt, lt), 0)
    c = lax.broadcasted_iota(jnp.int32, (lt, lt), 1)
    f = _dot_exact((c <= r).astype(F32), lf) + carry[...]
    carry[...] = f[lt - 1:lt, :]
    fa = jnp.concatenate(_split3(f * LOG2E), axis=1)
    aug_q = (_dot(fa, pq_ref[...]) + oq_ref[...]).astype(BF16)
    aug_k = (_dot(fa, pk_ref[...]) + ok_ref[...]).astype(BF16)
    vt = v.T.astype(BF16)
    upper_half = lax.broadcasted_iota(jnp.int32, (1, LANES), 1) >= HEAD_DIM
    ones = jnp.ones((BF16_SUBLANES, lt), BF16)
    for p in range(N_HEADS // 2):
        sl = slice(p * LANES, (p + 1) * LANES)
        ka_ref[p, :, 0:LANES] = kb[:, sl]
        ka_ref[p, :, LANES:2 * LANES] = aug_k[:, sl]
        for hh in range(2):
            h = 2 * p + hh
            keep = upper_half == (hh == 1)
            qa_ref[h, :, 0:LANES] = jnp.where(keep, q[:, sl], jnp.zeros_like(q[:, sl]))
            qa_ref[h, :, LANES:2 * LANES] = jnp.where(keep, aug_q[:, sl], jnp.zeros_like(aug_q[:, sl]))
            vt_ref[h, 0:HEAD_DIM, :] = vt[h * HEAD_DIM:(h + 1) * HEAD_DIM, :]
            vt_ref[h, HEAD_DIM:V_ROWS, :] = ones


def _placement():
    pq = np.zeros((3 * LANES, W_C), np.float32)
    pk = np.zeros((3 * LANES, W_C), np.float32)
    oq = np.zeros((1, W_C), np.float32)
    ok = np.zeros((1, W_C), np.float32)
    for h in range(N_HEADS):
        base = (h // 2) * LANES + (h % 2) * HEAD_DIM
        for part in range(3):
            pq[part * LANES + h, base + part] = 1.0
            pk[part * LANES + h, base + 3 + part] = -1.0
            ok[0, base + part] = 1.0
            oq[0, base + 3 + part] = 1.0
    return jnp.asarray(pq, BF16), jnp.asarray(pk, BF16), jnp.asarray(oq), jnp.asarray(ok)


def _odd_in(x, g, w, b_f, layer, j, prompt):
    rows, d = x.shape
    tm = min(ROW_TILE, rows)
    n = rows // tm
    d_in = w.shape[-1]
    n_pairs = N_HEADS // 2
    row = lambda i: (i, 0)
    lay3 = lambda i: (j, 0, 0)
    const = lambda i: (0, 0)
    in_specs = [
        pl.BlockSpec((tm, d), row),
        pl.BlockSpec((None, 1, d), lambda i: (layer, 0, 0)),
        pl.BlockSpec((None, d, d_in), lay3),
        pl.BlockSpec((None, 1, LANES), lay3),
    ]
    args = [x, g, w, b_f]
    common_specs = [pl.BlockSpec((tm, W_C), row), pl.BlockSpec((tm, W_C), row), pl.BlockSpec((tm, N_HEADS), row)]
    common_shape = [jax.ShapeDtypeStruct((rows, W_C), F32), jax.ShapeDtypeStruct((rows, W_C), F32),
                    jax.ShapeDtypeStruct((rows, N_HEADS), F32)]
    u_spec, u_shape = pl.BlockSpec((tm, W_D), row), jax.ShapeDtypeStruct((rows, W_D), F32)
    if prompt:
        in_specs += [pl.BlockSpec((3 * LANES, W_C), const), pl.BlockSpec((3 * LANES, W_C), const),
                     pl.BlockSpec((1, W_C), const), pl.BlockSpec((1, W_C), const)]
        args += list(_placement())
        out_specs = [pl.BlockSpec((N_HEADS, tm, 2 * LANES), lambda i: (0, i, 0)),
                     pl.BlockSpec((n_pairs, tm, 2 * LANES), lambda i: (0, i, 0)),
                     pl.BlockSpec((N_HEADS, V_ROWS, tm), lambda i: (0, 0, i))] + common_specs + [u_spec]
        out_shape = [jax.ShapeDtypeStruct((N_HEADS, rows, 2 * LANES), BF16),
                     jax.ShapeDtypeStruct((n_pairs, rows, 2 * LANES), BF16),
                     jax.ShapeDtypeStruct((N_HEADS, V_ROWS, rows), BF16)] + common_shape + [u_shape]
        scratch = [pltpu.VMEM((1, LANES), F32)]
    else:
        out_specs = [pl.BlockSpec((tm, W_C), row)] * 3 + common_specs + [pl.BlockSpec((tm, LANES), row), u_spec]
        out_shape = [jax.ShapeDtypeStruct((rows, W_C), BF16)] * 3 + common_shape + [
            jax.ShapeDtypeStruct((rows, LANES), F32), u_shape]
        scratch = []
    return pl.pallas_call(
        functools.partial(_odd_in_kernel, lt=tm, prompt=prompt),
        grid=(n,),
        in_specs=in_specs,
        out_specs=out_specs,
        out_shape=out_shape,
        scratch_shapes=scratch,
        compiler_params=_params(("arbitrary",)),
        name="odd_in",
    )(*args)


def _flash_kernel(qi_tab, ki_tab, qa_ref, ka_ref, vt_ref, o_ref, m_scr, acc_scr, *, tq, tk):
    t = pl.program_id(0)
    qi = qi_tab[t]
    ki = ki_tab[t]
    lead = qi * tq - ki * tk

    @pl.when(ki == 0)
    def _():
        m_scr[...] = jnp.full_like(m_scr, NEG_INF)
        acc_scr[...] = jnp.zeros_like(acc_scr)

    def head_scores(h):
        return _dot_nt(ka_ref[h // 2], qa_ref[h])

    def head_update(h, st, masked):
        if masked:
            key = lax.broadcasted_iota(jnp.int32, st.shape, 0)
            qry = lax.broadcasted_iota(jnp.int32, st.shape, 1)
            st = jnp.where(key <= (qry if tq == tk else qry + lead), st, NEG_INF)
        m_prev = m_scr[h]
        m_new = jnp.maximum(m_prev, jnp.max(st, axis=0, keepdims=True))
        alpha = jnp.exp2(m_prev - m_new)
        e = jnp.exp2(st - m_new).astype(BF16)
        acc_scr[h] = alpha * acc_scr[h] + _dot(vt_ref[h], e)
        m_scr[h] = m_new

    def step(masked):
        groups = [range(g, g + FLASH_GROUP) for g in range(0, N_HEADS, FLASH_GROUP)]
        scores = [head_scores(h) for h in groups[0]]
        for i, group in enumerate(groups):
            ahead = [head_scores(h) for h in groups[i + 1]] if i + 1 < len(groups) else None
            for h, st in zip(group, scores):
                head_update(h, st, masked)
            scores = ahead

    crosses_diagonal = lead < tk - 1

    @pl.when(jnp.logical_not(crosses_diagonal))
    def _():
        step(False)

    @pl.when(crosses_diagonal)
    def _():
        step(True)

    @pl.when(lead + tq == tk)
    def _():
        for p in range(N_HEADS // 2):
            pair_out = [acc_scr[h, 0:HEAD_DIM, :] / acc_scr[h, HEAD_DIM:HEAD_DIM + 1, :] for h in (2 * p, 2 * p + 1)]
            o_ref[:, p * LANES:(p + 1) * LANES] = jnp.concatenate(pair_out, axis=0).T.astype(BF16)


def _flash(qa, ka, vt):
    rows = qa.shape[1]
    n_pairs = N_HEADS // 2
    tk = min(ATT_KEY_TILE, rows)
    tq = min(ATT_QUERY_TILE, rows)
    nq = rows // tq
    k_per_q = tq // tk
    qi_tab = np.concatenate([np.full((i + 1) * k_per_q, i, np.int32) for i in range(nq)])
    ki_tab = np.concatenate([np.arange((i + 1) * k_per_q, dtype=np.int32) for i in range(nq)])
    grid_spec = pltpu.PrefetchScalarGridSpec(
        num_scalar_prefetch=2,
        grid=(qi_tab.shape[0],),
        in_specs=[
            pl.BlockSpec((N_HEADS, tq, 2 * LANES), lambda t, qt, kt: (0, qt[t], 0)),
            pl.BlockSpec((n_pairs, tk, 2 * LANES), lambda t, qt, kt: (0, kt[t], 0)),
            pl.BlockSpec((N_HEADS, V_ROWS, tk), lambda t, qt, kt: (0, 0, kt[t])),
        ],
        out_specs=pl.BlockSpec((tq, W_C), lambda t, qt, kt: (qt[t], 0)),
        scratch_shapes=[pltpu.VMEM((N_HEADS, 1, tq), F32), pltpu.VMEM((N_HEADS, V_ROWS, tq), F32)],
    )
    return pl.pallas_call(
        functools.partial(_flash_kernel, tq=tq, tk=tk),
        grid_spec=grid_spec,
        out_shape=jax.ShapeDtypeStruct((rows, W_C), BF16),
        compiler_params=_params(("arbitrary",)),
        name="flash",
    )(jnp.asarray(qi_tab), jnp.asarray(ki_tab), qa, ka, vt)


def _decode_kernel(q_ref, kn_ref, vn_ref, ck_ref, cv_ref, cl_ref, lfp_ref, o_ref, knew, vnew, lfa, gcol, grow,
                   *, past, length, s_pad):
    lane = lax.broadcasted_iota(jnp.int32, (1, LANES), 1)
    n_blk = s_pad // SCAN_BLOCK

    lfa[...] = jnp.zeros_like(lfa)
    lfa[0:past, 0:N_HEADS] = cl_ref[...]
    lfa[past:past + length, :] = lfp_ref[...]
    r = lax.broadcasted_iota(jnp.int32, (SCAN_BLOCK, SCAN_BLOCK), 0)
    c = lax.broadcasted_iota(jnp.int32, (SCAN_BLOCK, SCAN_BLOCK), 1)
    upper = (c > r).astype(F32)
    carry = jnp.zeros((1, LANES), F32)
    for b in reversed(range(n_blk)):
        blk = lfa[b * SCAN_BLOCK:(b + 1) * SCAN_BLOCK, :]
        g = _dot_exact(upper, blk) + carry
        carry = carry + jnp.sum(blk, axis=0, keepdims=True)
        gcol[b * SCAN_BLOCK:(b + 1) * SCAN_BLOCK, :] = g
        grow[:, b * SCAN_BLOCK:(b + 1) * SCAN_BLOCK] = g.T

    pad = jnp.zeros((LANES - length, LANES), BF16)
    second = lax.broadcasted_iota(jnp.int32, (2 * length, 1), 0) >= length
    row = lax.broadcasted_iota(jnp.int32, (2 * length, LANES), 0)
    col = lax.broadcasted_iota(jnp.int32, (2 * length, LANES), 1)
    causal = col <= jnp.where(row >= length, row - length, row)
    g_own = gcol[past:past + length, :]
    for p in range(N_HEADS // 2):
        sl = slice(p * LANES, (p + 1) * LANES)
        knew[0:length, :] = kn_ref[:, sl]
        knew[length:LANES, :] = pad
        vnew[0:length, :] = vn_ref[:, sl]
        vnew[length:LANES, :] = pad
        q = q_ref[:, sl]
        q2 = jnp.concatenate([jnp.where(lane < HEAD_DIM, q, jnp.zeros_like(q)),
                              jnp.where(lane >= HEAD_DIM, q, jnp.zeros_like(q))], axis=0)
        gq = jnp.concatenate([jnp.sum(jnp.where(lane == 2 * p + hh, g_own, 0.0), axis=1, keepdims=True)
                              for hh in range(2)], axis=0)
        bias = jnp.where(second, grow[2 * p + 1:2 * p + 2, :], grow[2 * p:2 * p + 1, :]) - gq
        s_c = _dot_nt(q2, ck_ref[:, sl]) + bias[:, 0:past]
        s_n = jnp.where(causal, _dot_nt(q2, knew[...]) + bias[:, past:s_pad], NEG_INF)
        m = jnp.maximum(jnp.max(s_c, axis=1, keepdims=True), jnp.max(s_n, axis=1, keepdims=True))
        e_c = jnp.exp(s_c - m)
        e_n = jnp.exp(s_n - m)
        l = jnp.sum(e_c, axis=1, keepdims=True) + jnp.sum(e_n, axis=1, keepdims=True)
        out = (_dot(e_c.astype(BF16), cv_ref[:, sl]) + _dot(e_n.astype(BF16), vnew[...])) / l
        o_ref[:, sl] = jnp.where(lane < HEAD_DIM, out[0:length], out[length:2 * length]).astype(BF16)


def _decode(q, kn, vn, cache_k, cache_v, cache_logf, lfp, batch, length, j):
    rows = q.shape[0]
    past = cache_k.shape[2]
    s_pad = past + LANES
    ck = cache_k.reshape(cache_k.shape[0] * batch, past, W_C).astype(BF16)
    cv = cache_v.reshape(cache_v.shape[0] * batch, past, W_C).astype(BF16)
    cl = cache_logf.reshape(cache_logf.shape[0] * batch, past, N_HEADS)
    new_map = lambda b: (b, 0)
    cache_map = lambda b: (j * batch + b, 0, 0)
    return pl.pallas_call(
        functools.partial(_decode_kernel, past=past, length=length, s_pad=s_pad),
        grid=(batch,),
        in_specs=[
            pl.BlockSpec((length, W_C), new_map),
            pl.BlockSpec((length, W_C), new_map),
            pl.BlockSpec((length, W_C), new_map),
            pl.BlockSpec((None, past, W_C), cache_map),
            pl.BlockSpec((None, past, W_C), cache_map),
            pl.BlockSpec((None, past, N_HEADS), cache_map),
            pl.BlockSpec((length, LANES), new_map),
        ],
        out_specs=pl.BlockSpec((length, W_C), new_map),
        out_shape=jax.ShapeDtypeStruct((rows, W_C), BF16),
        scratch_shapes=[
            pltpu.VMEM((LANES, LANES), BF16), pltpu.VMEM((LANES, LANES), BF16),
            pltpu.VMEM((s_pad, LANES), F32), pltpu.VMEM((s_pad, LANES), F32), pltpu.VMEM((LANES, s_pad), F32),
        ],
        compiler_params=_params(("arbitrary",)),
        name="decode",
    )(q, kn, vn, ck, cv, cl, lfp)


def _odd_out_kernel(o_ref, u_ref, x_ref, pp_ref, pw_ref, ps_ref, wout_ref, y_ref, np_ref, ext, *, nb, lt, n_l, pos0):
    j = pl.program_id(1)
    hist = POOL_MAX - 1

    @pl.when(j == 0)
    def _():
        ext[:, HALO_P - hist:HALO_P, :] = pp_ref[...]

    if n_l > 1:
        @pl.when(j > 0)
        def _():
            ext[:, 0:HALO_P, :] = ext[:, lt:lt + HALO_P, :]

    u = u_ref[...].reshape(nb, lt, W_D)
    ext[:, HALO_P:HALO_P + lt, :] = u
    group = lax.broadcasted_iota(jnp.int32, (nb, lt, W_D), 2) // GROUP_DIM
    acc = u
    win = None
    for back in range(1, POOL_MAX):
        acc = acc + ext[:, HALO_P - back:HALO_P - back + lt, :]
        if back + 1 in POOL_WINDOWS:
            g = POOL_WINDOWS.index(back + 1)
            win = acc if win is None else jnp.where(group >= g, acc, win)
    width = jnp.left_shift(2, group)
    pos = pos0 + j * lt + lax.broadcasted_iota(jnp.int32, (nb, lt, W_D), 1)
    cnt = jnp.minimum(width, pos + 1).astype(F32)
    z = (win / cnt - u).reshape(nb * lt, W_D)
    yd = _dot(z.astype(BF16), pw_ref[...]) * ps_ref[...]
    y = _dot(o_ref[...], wout_ref[0:W_C, :]) + _dot(yd.astype(BF16), wout_ref[W_C:W_C + W_D, :])
    y_ref[...] = x_ref[...] + y

    @pl.when(j == n_l - 1)
    def _():
        np_ref[...] = ext[:, HALO_P + lt - hist:HALO_P + lt, :]


def _odd_out(o, u, x, batch, length, pos0, prev_pool, pool_w, pool_scale, w_out, j):
    rows, d = x.shape
    nb, lt, n_l = _seq_tiling(batch, length)
    tm = nb * lt
    row_map = lambda b, t: (b * n_l + t, 0)
    lay3 = lambda b, t: (j, 0, 0)
    seq_map = lambda b, t: (b, 0, 0)
    return pl.pallas_call(
        functools.partial(_odd_out_kernel, nb=nb, lt=lt, n_l=n_l, pos0=pos0),
        grid=(batch // nb, n_l),
        in_specs=[
            pl.BlockSpec((tm, W_C), row_map),
            pl.BlockSpec((tm, W_D), row_map),
            pl.BlockSpec((tm, d), row_map),
            pl.BlockSpec((nb, POOL_MAX - 1, W_D), seq_map),
            pl.BlockSpec((None, W_D, W_D), lay3),
            pl.BlockSpec((None, 1, W_D), lay3),
            pl.BlockSpec((None, W_C + W_D, d), lay3),
        ],
        out_specs=[pl.BlockSpec((tm, d), row_map), pl.BlockSpec((nb, POOL_MAX - 1, W_D), seq_map)],
        out_shape=[jax.ShapeDtypeStruct((rows, d), F32), jax.ShapeDtypeStruct((batch, POOL_MAX - 1, W_D), F32)],
        scratch_shapes=[pltpu.VMEM((nb, HALO_P + lt, W_D), F32)],
        compiler_params=_params(("arbitrary", "arbitrary")),
        name="odd_out",
    )(o, u, x, prev_pool, pool_w, pool_scale, w_out)


def _trunk(x3, prev_a, prev_b, cache, prev_pool, W):
    batch, length, d = x3.shape
    depth = W['w_ffn1_in'].shape[0]
    x = x3.reshape(batch * length, d)
    pos0 = 0 if cache is None else cache[0].shape[2]
    outs = {n: [] for n in 'abkvfp'}
    for i in range(depth):
        j = i // 2
        x = _ffn(x, W['norm_ffn1'], W['w_ffn1_in'], W['w_ffn1_out'], i)
        if i % 2 == 0:
            x, sa, sb = _even(x, batch, length, W['norm_mix'], W['w_in_even'], W['conv_a_w'], W['conv_b_w'],
                              W['conv_b_bias'], W['ln_b_gain'], W['ln_b_bias'], W['w_out_even'],
                              prev_a[j], prev_b[j], i, j)
            outs['a'].append(sa)
            outs['b'].append(sb)
        else:
            res = _odd_in(x, W['norm_mix'], W['w_in_odd'], W['b_forget'], i, j, prompt=cache is None)
            if cache is None:
                qa, ka, vt, k, v, lf, u = res
                o = _flash(qa, ka, vt)
            else:
                q, kb, vb, k, v, lf, lfp, u = res
                o = _decode(q, kb, vb, cache[0], cache[1], cache[2], lfp, batch, length, j)
            x, sp = _odd_out(o, u, x, batch, length, pos0, prev_pool[j], W['pool_w'], W['pool_scale'],
                             W['w_out_odd'], j)
            outs['k'].append(k.reshape(batch, length, N_HEADS, HEAD_DIM))
            outs['v'].append(v.reshape(batch, length, N_HEADS, HEAD_DIM))
            outs['f'].append(lf.reshape(batch, length, N_HEADS))
            outs['p'].append(sp)
        x = _ffn(x, W['norm_ffn2'], W['w_ffn2_in'], W['w_ffn2_out'], i,
                 g_final=W['norm_final'] if i == depth - 1 else None)
    return (x.reshape(batch, length, d),) + tuple(jnp.stack(outs[n]) for n in 'abkvfp')


def kernel(x_prompt, x_sample, state_conv_a, state_conv_b, cache_k, cache_v, cache_logf, state_pool, norm_ffn1, w_ffn1_in, w_ffn1_out, norm_mix, w_in_even, conv_a_w, conv_b_w, conv_b_bias, ln_b_gain, ln_b_bias, w_out_even, w_in_odd, b_forget, pool_w, pool_scale, w_out_odd, norm_ffn2, w_ffn2_in, w_ffn2_out, norm_final):
    depth, d = norm_ffn1.shape
    n_odd = w_in_odd.shape[0]
    w_odd = jnp.concatenate([
        w_in_odd[:, :, :3 * W_C],
        jnp.pad(w_in_odd[:, :, 3 * W_C:3 * W_C + N_HEADS], ((0, 0), (0, 0), (0, LANES - N_HEADS))),
        w_in_odd[:, :, 3 * W_C + N_HEADS:],
    ], axis=-1).astype(BF16)
    pool_bd = jnp.zeros((n_odd, W_D, W_D), F32)
    for g in range(len(POOL_WINDOWS)):
        sl = slice(g * GROUP_DIM, (g + 1) * GROUP_DIM)
        pool_bd = pool_bd.at[:, sl, sl].set(pool_w[:, g])
    W = {
        'norm_ffn1': norm_ffn1.reshape(depth, 1, d), 'norm_ffn2': norm_ffn2.reshape(depth, 1, d),
        'norm_mix': norm_mix.reshape(depth, 1, d), 'norm_final': norm_final.reshape(1, d),
        'w_ffn1_in': w_ffn1_in.astype(BF16), 'w_ffn1_out': w_ffn1_out.astype(BF16),
        'w_ffn2_in': w_ffn2_in.astype(BF16), 'w_ffn2_out': w_ffn2_out.astype(BF16),
        'w_in_even': w_in_even.astype(BF16), 'w_out_even': w_out_even.astype(BF16),
        'conv_a_w': conv_a_w, 'conv_b_w': conv_b_w,
        'conv_b_bias': conv_b_bias[:, None, :], 'ln_b_gain': ln_b_gain[:, None, :], 'ln_b_bias': ln_b_bias[:, None, :],
        'w_in_odd': w_odd, 'w_out_odd': w_out_odd.astype(BF16),
        'b_forget': jnp.pad(b_forget, ((0, 0), (0, LANES - N_HEADS)))[:, None, :],
        'pool_w': pool_bd.astype(BF16), 'pool_scale': pool_scale[:, None, :],
    }
    b_p = x_prompt.shape[0]
    zeros = lambda n, r, c: jnp.zeros((n, b_p, r, c), x_prompt.dtype)
    n_even = w_in_even.shape[0]
    prompt = _trunk(x_prompt, zeros(n_even, K_A - 1, W_A), zeros(n_even, K_B - 1, W_B), None,
                    zeros(n_odd, POOL_MAX - 1, W_D), W)
    sample = _trunk(x_sample, state_conv_a, state_conv_b, (cache_k, cache_v, cache_logf), state_pool, W)
    return (prompt[0], sample[0]) + prompt[1:] + sample[1:]
```

```python
import functools

import numpy as np
import jax
import jax.numpy as jnp
from jax import lax
from jax.experimental import pallas as pl
from jax.experimental.pallas import tpu as pltpu

F32 = jnp.float32
BF16 = jnp.bfloat16

EPS = 1e-6
NEG_INF = -1e30
GROUP_DIM = 64
HEAD_DIM = 64
N_HEADS = 12
W_A = 512
W_B = 512
K_A = 3
K_B = 31
W_C = N_HEADS * HEAD_DIM
W_D = 256
POOL_WINDOWS = (2, 4, 8, 16)
POOL_MAX = 16
LANES = 128
SUBLANES = 8
BF16_SUBLANES = 16
V_ROWS = HEAD_DIM + BF16_SUBLANES
HALO_A = 8
HALO_B = 32
HALO_P = 16
VMEM_LIMIT = 56 * 1024 * 1024

ROW_TILE = 512
ATT_KEY_TILE = 512
ATT_QUERY_TILE = 512
FF_CHUNK = 256
SCAN_BLOCK = 128
FLASH_GROUP = 2
LOG2E = 1.4426950408889634


def _rms(x, g):
    return (x * lax.rsqrt(jnp.mean(x * x, axis=-1, keepdims=True) + EPS)) * g


def _dot(a, b):
    return jnp.dot(a, b, preferred_element_type=F32)


def _dot_nt(a, b):
    return lax.dot_general(a, b, (((1,), (1,)), ((), ())), preferred_element_type=F32)


def _dot_exact(a, b):
    return lax.dot_general(a, b, (((1,), (0,)), ((), ())), precision=lax.Precision.HIGHEST,
                           preferred_element_type=F32)


def _params(semantics):
    return pltpu.CompilerParams(dimension_semantics=semantics, vmem_limit_bytes=VMEM_LIMIT)


def _ffn_kernel(*refs, n_chunks, final, tiles):
    n = len(tiles)
    x_refs = refs[:n]
    g_ref, wg_ref, wu_ref, wo_ref = refs[n:n + 4]
    gf_ref = refs[n + 4] if final else None
    o_refs = refs[n + 4 + final:2 * n + 4 + final]
    h_scr, acc_scr = refs[2 * n + 4 + final:]

    def body(x_ref, o_ref, tm):
        h_scr[0:tm, :] = _rms(x_ref[...], g_ref[...]).astype(BF16)
        for c in range(n_chunks):
            sl = slice(c * FF_CHUNK, (c + 1) * FF_CHUNK)
            hb = h_scr[0:tm, :]
            gate = _dot(hb, wg_ref[:, sl])
            up = _dot(hb, wu_ref[:, sl])
            act = ((gate * jax.nn.sigmoid(gate)) * up).astype(BF16)
            part = _dot(act, wo_ref[sl, :])
            if c == 0:
                acc_scr[0:tm, :] = part
            else:
                acc_scr[0:tm, :] += part
        y = x_ref[...] + 0.5 * acc_scr[0:tm, :]
        if final:
            y = _rms(y, gf_ref[...])
        o_ref[...] = y

    i = pl.program_id(0)
    start = 0
    for x_ref, o_ref, (tm, count) in zip(x_refs, o_refs, tiles):
        if n == 1:
            body(x_ref, o_ref, tm)
        else:
            pl.when(jnp.logical_and(i >= start, i < start + count))(functools.partial(body, x_ref, o_ref, tm))
        start += count


def _ffn(xs, g, w_in, w_out, layer, g_final=None):
    d = xs[0].shape[1]
    d_ff = w_out.shape[1]
    final = g_final is not None
    tiles, row_specs, start = [], [], 0
    for x in xs:
        tm = min(ROW_TILE, x.shape[0])
        count = x.shape[0] // tm
        tiles.append((tm, count))
        row_specs.append(pl.BlockSpec((tm, d), lambda i, s=start, c=count: (jnp.clip(i - s, 0, c - 1), 0)))
        start += count
    in_specs = row_specs + [
        pl.BlockSpec((None, 1, d), lambda i: (layer, 0, 0)),
        pl.BlockSpec((None, d, d_ff), lambda i: (layer, 0, 0)),
        pl.BlockSpec((None, d, d_ff), lambda i: (layer, 0, 1)),
        pl.BlockSpec((None, d_ff, d), lambda i: (layer, 0, 0)),
    ]
    args = list(xs) + [g, w_in, w_in, w_out]
    if final:
        in_specs.append(pl.BlockSpec((1, d), lambda i: (0, 0)))
        args.append(g_final)
    tm_max = max(tm for tm, _ in tiles)
    return pl.pallas_call(
        functools.partial(_ffn_kernel, n_chunks=d_ff // FF_CHUNK, final=final, tiles=tuple(tiles)),
        grid=(start,),
        in_specs=in_specs,
        out_specs=row_specs,
        out_shape=[jax.ShapeDtypeStruct(x.shape, F32) for x in xs],
        scratch_shapes=[pltpu.VMEM((tm_max, d), BF16), pltpu.VMEM((tm_max, d), F32)],
        compiler_params=_params(("arbitrary",)),
        name="ffn",
    )(*args)


def _even_kernel(x_ref, g_ref, win_ref, wa_ref, wb_ref, bb_ref, lng_ref, lnb_ref, wout_ref, pa_ref, pb_ref,
                 o_ref, na_ref, nb_ref, ext_a, ext_b, shift_scr, *, nb, lt, n_l):
    j = pl.program_id(1)

    @pl.when(j == 0)
    def _():
        ext_a[:, HALO_A - (K_A - 1):HALO_A, :] = pa_ref[...]
        ext_b[:, HALO_B - (K_B - 1):HALO_B, :] = pb_ref[...]

    if n_l > 1:
        @pl.when(j > 0)
        def _():
            ext_a[:, 0:HALO_A, :] = ext_a[:, lt:lt + HALO_A, :]
            ext_b[:, 0:HALO_B, :] = ext_b[:, lt:lt + HALO_B, :]

    x = x_ref[...]
    h = _rms(x, g_ref[...]).astype(BF16)

    def proj(i, width):
        return _dot(h, win_ref[:, i * width:(i + 1) * width])

    ext_a[:, HALO_A:HALO_A + lt, :] = (proj(1, W_A) * proj(2, W_A)).reshape(nb, lt, W_A)
    ext_b[:, HALO_B:HALO_B + lt, :] = (proj(3, W_B) * jax.nn.sigmoid(proj(4, W_B))).reshape(nb, lt, W_B)

    def conv(ext, w_ref, taps, halo):
        base = halo - (taps - 1)
        acc = None
        for r in range(SUBLANES):
            offs = [base + k for k in range(taps) if (base + k) % SUBLANES == r]
            if not offs:
                continue
            if len(offs) == 1 or r == 0:
                windows = [ext[:, off:off + lt, :] for off in offs]
            else:
                buf = shift_scr.at[r % 2]
                span = offs[-1] - offs[0] + lt
                buf[:, 0:span, :] = ext[:, offs[0]:offs[0] + span, :]
                windows = [buf[:, off - offs[0]:off - offs[0] + lt, :] for off in offs]
            for off, win in zip(offs, windows):
                term = w_ref[off - base:off - base + 1, :] * win
                acc = term if acc is None else acc + term
        return acc.reshape(nb * lt, acc.shape[-1])

    ya = proj(0, W_A) * conv(ext_a, wa_ref, K_A, HALO_A)
    yc = conv(ext_b, wb_ref, K_B, HALO_B) + bb_ref[...]
    mu = jnp.mean(yc, axis=-1, keepdims=True)
    dev = yc - mu
    var = jnp.mean(dev * dev, axis=-1, keepdims=True)
    yn = dev * lax.rsqrt(var + EPS) * lng_ref[...] + lnb_ref[...]
    yb = yn * jax.nn.sigmoid(yn)
    y = _dot(ya.astype(BF16), wout_ref[0:W_A, :]) + _dot(yb.astype(BF16), wout_ref[W_A:W_A + W_B, :])
    o_ref[...] = x + y

    @pl.when(j == n_l - 1)
    def _():
        na_ref[...] = ext_a[:, HALO_A + lt - (K_A - 1):HALO_A + lt, :]
        nb_ref[...] = ext_b[:, HALO_B + lt - (K_B - 1):HALO_B + lt, :]


def _seq_tiling(batch, length):
    if length >= ROW_TILE:
        return 1, ROW_TILE, length // ROW_TILE
    nb = max(1, min(batch, ROW_TILE // length))
    return nb, length, 1


def _even(x, batch, length, g, w_in, conv_a, conv_b, bias_b, ln_g, ln_b, w_out, prev_a, prev_b, layer, j):
    rows, d = x.shape
    nb, lt, n_l = _seq_tiling(batch, length)
    tm = nb * lt
    d_in = w_in.shape[-1]
    row_map = lambda b, t: (b * n_l + t, 0)
    lay3 = lambda b, t: (j, 0, 0)
    seq_map = lambda b, t: (b, 0, 0)
    return pl.pallas_call(
        functools.partial(_even_kernel, nb=nb, lt=lt, n_l=n_l),
        grid=(batch // nb, n_l),
        in_specs=[
            pl.BlockSpec((tm, d), row_map),
            pl.BlockSpec((None, 1, d), lambda b, t: (layer, 0, 0)),
            pl.BlockSpec((None, d, d_in), lay3),
            pl.BlockSpec((None, K_A, W_A), lay3),
            pl.BlockSpec((None, K_B, W_B), lay3),
            pl.BlockSpec((None, 1, W_B), lay3),
            pl.BlockSpec((None, 1, W_B), lay3),
            pl.BlockSpec((None, 1, W_B), lay3),
            pl.BlockSpec((None, W_A + W_B, d), lay3),
            pl.BlockSpec((nb, K_A - 1, W_A), seq_map),
            pl.BlockSpec((nb, K_B - 1, W_B), seq_map),
        ],
        out_specs=[
            pl.BlockSpec((tm, d), row_map),
            pl.BlockSpec((nb, K_A - 1, W_A), seq_map),
            pl.BlockSpec((nb, K_B - 1, W_B), seq_map),
        ],
        out_shape=[
            jax.ShapeDtypeStruct((rows, d), F32),
            jax.ShapeDtypeStruct((batch, K_A - 1, W_A), F32),
            jax.ShapeDtypeStruct((batch, K_B - 1, W_B), F32),
        ],
        scratch_shapes=[pltpu.VMEM((nb, HALO_A + lt, W_A), F32), pltpu.VMEM((nb, HALO_B + lt, W_B), F32),
                        pltpu.VMEM((2, nb, HALO_B + lt, W_B), F32)],
        compiler_params=_params(("arbitrary", "arbitrary")),
        name="even",
    )(x, g, w_in, conv_a, conv_b, bias_b, ln_g, ln_b, w_out, prev_a, prev_b)


def _split3(f):
    hi = f.astype(BF16)
    r1 = f - hi.astype(F32)
    mid = r1.astype(BF16)
    lo = (r1 - mid.astype(F32)).astype(BF16)
    return hi, mid, lo


def _odd_in_kernel(*refs, lt, prompt):
    x_ref, g_ref, w_ref, bf_ref = refs[:4]
    if prompt:
        pq_ref, pk_ref, oq_ref, ok_ref = refs[4:8]
        qa_ref, ka_ref, vt_ref, k_ref, v_ref, lf_ref, u_ref, carry = refs[8:]
    else:
        q_ref, kb_ref, vb_ref, k_ref, v_ref, lf_ref, lfp_ref, u_ref = refs[4:]
    h = _rms(x_ref[...], g_ref[...]).astype(BF16)
    q_scale = HEAD_DIM ** -0.5 * (LOG2E if prompt else 1.0)
    q = (_dot(h, w_ref[:, 0:W_C]) * q_scale).astype(BF16)
    k = _dot(h, w_ref[:, W_C:2 * W_C])
    k_ref[...] = k
    kb = k.astype(BF16)
    v = _dot(h, w_ref[:, 2 * W_C:3 * W_C])
    v_ref[...] = v
    z = _dot(h, w_ref[:, 3 * W_C:3 * W_C + LANES]) + bf_ref[...]
    lf = jnp.minimum(z, 0.0) - jnp.log1p(jnp.exp(-jnp.abs(z)))
    lf_ref[...] = lf[:, :N_HEADS]
    u_ref[...] = _dot(h, w_ref[:, 3 * W_C + LANES:3 * W_C + LANES + W_D])
    if not prompt:
        q_ref[...] = q
        kb_ref[...] = kb
        vb_ref[...] = v.astype(BF16)
        lfp_ref[...] = lf
        return

    @pl.when(pl.program_id(0) == 0)
    def _():
        carry[...] = jnp.zeros_like(carry)

    r = lax.broadcasted_iota(jnp.int32, (lt, lt), 0)
    c = lax.broadcasted_iota(jnp.int32, (lt, lt), 1)
    f = _dot_exact((c <= r).astype(F32), lf) + carry[...]
    carry[...] = f[lt - 1:lt, :]
    fa = jnp.concatenate(_split3(f * LOG2E), axis=1)
    aug_q = (_dot(fa, pq_ref[...]) + oq_ref[...]).astype(BF16)
    aug_k = (_dot(fa, pk_ref[...]) + ok_ref[...]).astype(BF16)
    vt = v.T.astype(BF16)
    upper_half = lax.broadcasted_iota(jnp.int32, (1, LANES), 1) >= HEAD_DIM
    ones = jnp.ones((BF16_SUBLANES, lt), BF16)
    for p in range(N_HEADS // 2):
        sl = slice(p * LANES, (p + 1) * LANES)
        ka_ref[p, :, 0:LANES] = kb[:, sl]
        ka_ref[p, :, LANES:2 * LANES] = aug_k[:, sl]
        for hh in range(2):
            h = 2 * p + hh
            keep = upper_half == (hh == 1)
            qa_ref[h, :, 0:LANES] = jnp.where(keep, q[:, sl], jnp.zeros_like(q[:, sl]))
            qa_ref[h, :, LANES:2 * LANES] = jnp.where(keep, aug_q[:, sl], jnp.zeros_like(aug_q[:, sl]))
            vt_ref[h, 0:HEAD_DIM, :] = vt[h * HEAD_DIM:(h + 1) * HEAD_DIM, :]
            vt_ref[h, HEAD_DIM:V_ROWS, :] = ones


def _placement():
    pq = np.zeros((3 * LANES, W_C), np.float32)
    pk = np.zeros((3 * LANES, W_C), np.float32)
    oq = np.zeros((1, W_C), np.float32)
    ok = np.zeros((1, W_C), np.float32)
    for h in range(N_HEADS):
        base = (h // 2) * LANES + (h % 2) * HEAD_DIM
        for part in range(3):
            pq[part * LANES + h, base + part] = 1.0
            pk[part * LANES + h, base + 3 + part] = -1.0
            ok[0, base + part] = 1.0
            oq[0, base + 3 + part] = 1.0
    return jnp.asarray(pq, BF16), jnp.asarray(pk, BF16), jnp.asarray(oq), jnp.asarray(ok)


def _odd_in(x, g, w, b_f, layer, j, prompt):
    rows, d = x.shape
    tm = min(ROW_TILE, rows)
    n = rows // tm
    d_in = w.shape[-1]
    n_pairs = N_HEADS // 2
    row = lambda i: (i, 0)
    lay3 = lambda i: (j, 0, 0)
    const = lambda i: (0, 0)
    in_specs = [
        pl.BlockSpec((tm, d), row),
        pl.BlockSpec((None, 1, d), lambda i: (layer, 0, 0)),
        pl.BlockSpec((None, d, d_in), lay3),
        pl.BlockSpec((None, 1, LANES), lay3),
    ]
    args = [x, g, w, b_f]
    common_specs = [pl.BlockSpec((tm, W_C), row), pl.BlockSpec((tm, W_C), row), pl.BlockSpec((tm, N_HEADS), row)]
    common_shape = [jax.ShapeDtypeStruct((rows, W_C), F32), jax.ShapeDtypeStruct((rows, W_C), F32),
                    jax.ShapeDtypeStruct((rows, N_HEADS), F32)]
    u_spec, u_shape = pl.BlockSpec((tm, W_D), row), jax.ShapeDtypeStruct((rows, W_D), F32)
    if prompt:
        in_specs += [pl.BlockSpec((3 * LANES, W_C), const), pl.BlockSpec((3 * LANES, W_C), const),
                     pl.BlockSpec((1, W_C), const), pl.BlockSpec((1, W_C), const)]
        args += list(_placement())
        out_specs = [pl.BlockSpec((N_HEADS, tm, 2 * LANES), lambda i: (0, i, 0)),
                     pl.BlockSpec((n_pairs, tm, 2 * LANES), lambda i: (0, i, 0)),
                     pl.BlockSpec((N_HEADS, V_ROWS, tm), lambda i: (0, 0, i))] + common_specs + [u_spec]
        out_shape = [jax.ShapeDtypeStruct((N_HEADS, rows, 2 * LANES), BF16),
                     jax.ShapeDtypeStruct((n_pairs, rows, 2 * LANES), BF16),
                     jax.ShapeDtypeStruct((N_HEADS, V_ROWS, rows), BF16)] + common_shape + [u_shape]
        scratch = [pltpu.VMEM((1, LANES), F32)]
    else:
        out_specs = [pl.BlockSpec((tm, W_C), row)] * 3 + common_specs + [pl.BlockSpec((tm, LANES), row), u_spec]
        out_shape = [jax.ShapeDtypeStruct((rows, W_C), BF16)] * 3 + common_shape + [
            jax.ShapeDtypeStruct((rows, LANES), F32), u_shape]
        scratch = []
    return pl.pallas_call(
        functools.partial(_odd_in_kernel, lt=tm, prompt=prompt),
        grid=(n,),
        in_specs=in_specs,
        out_specs=out_specs,
        out_shape=out_shape,
        scratch_shapes=scratch,
        compiler_params=_params(("arbitrary",)),
        name="odd_in",
    )(*args)


def _flash_kernel(qi_tab, ki_tab, qa_ref, ka_ref, vt_ref, o_ref, m_scr, acc_scr, *, tq, tk):
    t = pl.program_id(0)
    qi = qi_tab[t]
    ki = ki_tab[t]
    lead = qi * tq - ki * tk

    @pl.when(ki == 0)
    def _():
        m_scr[...] = jnp.full_like(m_scr, NEG_INF)
        acc_scr[...] = jnp.zeros_like(acc_scr)

    def head_scores(h):
        return _dot_nt(ka_ref[h // 2], qa_ref[h])

    def head_update(h, st, masked):
        if masked:
            key = lax.broadcasted_iota(jnp.int32, st.shape, 0)
            qry = lax.broadcasted_iota(jnp.int32, st.shape, 1)
            st = jnp.where(key <= (qry if tq == tk else qry + lead), st, NEG_INF)
        m_prev = m_scr[h]
        m_new = jnp.maximum(m_prev, jnp.max(st, axis=0, keepdims=True))
        alpha = jnp.exp2(m_prev - m_new)
        e = jnp.exp2(st - m_new).astype(BF16)
        acc_scr[h] = alpha * acc_scr[h] + _dot(vt_ref[h], e)
        m_scr[h] = m_new

    def step(masked):
        groups = [range(g, g + FLASH_GROUP) for g in range(0, N_HEADS, FLASH_GROUP)]
        scores = [head_scores(h) for h in groups[0]]
        for i, group in enumerate(groups):
            ahead = [head_scores(h) for h in groups[i + 1]] if i + 1 < len(groups) else None
            for h, st in zip(group, scores):
                head_update(h, st, masked)
            scores = ahead

    crosses_diagonal = lead < tk - 1

    @pl.when(jnp.logical_not(crosses_diagonal))
    def _():
        step(False)

    @pl.when(crosses_diagonal)
    def _():
        step(True)

    @pl.when(lead + tq == tk)
    def _():
        for p in range(N_HEADS // 2):
            pair_out = [acc_scr[h, 0:HEAD_DIM, :] / acc_scr[h, HEAD_DIM:HEAD_DIM + 1, :] for h in (2 * p, 2 * p + 1)]
            o_ref[:, p * LANES:(p + 1) * LANES] = jnp.concatenate(pair_out, axis=0).T.astype(BF16)


def _flash(qa, ka, vt):
    rows = qa.shape[1]
    n_pairs = N_HEADS // 2
    tk = min(ATT_KEY_TILE, rows)
    tq = min(ATT_QUERY_TILE, rows)
    nq = rows // tq
    k_per_q = tq // tk
    qi_tab = np.concatenate([np.full((i + 1) * k_per_q, i, np.int32) for i in range(nq)])
    ki_tab = np.concatenate([np.arange((i + 1) * k_per_q, dtype=np.int32) for i in range(nq)])
    grid_spec = pltpu.PrefetchScalarGridSpec(
        num_scalar_prefetch=2,
        grid=(qi_tab.shape[0],),
        in_specs=[
            pl.BlockSpec((N_HEADS, tq, 2 * LANES), lambda t, qt, kt: (0, qt[t], 0)),
            pl.BlockSpec((n_pairs, tk, 2 * LANES), lambda t, qt, kt: (0, kt[t], 0)),
            pl.BlockSpec((N_HEADS, V_ROWS, tk), lambda t, qt, kt: (0, 0, kt[t])),
        ],
        out_specs=pl.BlockSpec((tq, W_C), lambda t, qt, kt: (qt[t], 0)),
        scratch_shapes=[pltpu.VMEM((N_HEADS, 1, tq), F32), pltpu.VMEM((N_HEADS, V_ROWS, tq), F32)],
    )
    return pl.pallas_call(
        functools.partial(_flash_kernel, tq=tq, tk=tk),
        grid_spec=grid_spec,
        out_shape=jax.ShapeDtypeStruct((rows, W_C), BF16),
        compiler_params=_params(("arbitrary",)),
        name="flash",
    )(jnp.asarray(qi_tab), jnp.asarray(ki_tab), qa, ka, vt)


def _decode_kernel(q_ref, kn_ref, vn_ref, ck_ref, cv_ref, cl_ref, lfp_ref, o_ref, knew, vnew, lfa, gcol, grow,
                   *, past, length, s_pad):
    lane = lax.broadcasted_iota(jnp.int32, (1, LANES), 1)
    n_blk = s_pad // SCAN_BLOCK

    lfa[...] = jnp.zeros_like(lfa)
    lfa[0:past, 0:N_HEADS] = cl_ref[...]
    lfa[past:past + length, :] = lfp_ref[...]
    r = lax.broadcasted_iota(jnp.int32, (SCAN_BLOCK, SCAN_BLOCK), 0)
    c = lax.broadcasted_iota(jnp.int32, (SCAN_BLOCK, SCAN_BLOCK), 1)
    upper = (c > r).astype(F32)
    carry = jnp.zeros((1, LANES), F32)
    for b in reversed(range(n_blk)):
        blk = lfa[b * SCAN_BLOCK:(b + 1) * SCAN_BLOCK, :]
        g = _dot_exact(upper, blk) + carry
        carry = carry + jnp.sum(blk, axis=0, keepdims=True)
        gcol[b * SCAN_BLOCK:(b + 1) * SCAN_BLOCK, :] = g
        grow[:, b * SCAN_BLOCK:(b + 1) * SCAN_BLOCK] = g.T

    pad = jnp.zeros((LANES - length, LANES), BF16)
    second = lax.broadcasted_iota(jnp.int32, (2 * length, 1), 0) >= length
    row = lax.broadcasted_iota(jnp.int32, (2 * length, LANES), 0)
    col = lax.broadcasted_iota(jnp.int32, (2 * length, LANES), 1)
    causal = col <= jnp.where(row >= length, row - length, row)
    g_own = gcol[past:past + length, :]
    for p in range(N_HEADS // 2):
        sl = slice(p * LANES, (p + 1) * LANES)
        knew[0:length, :] = kn_ref[:, sl]
        knew[length:LANES, :] = pad
        vnew[0:length, :] = vn_ref[:, sl]
        vnew[length:LANES, :] = pad
        q = q_ref[:, sl]
        q2 = jnp.concatenate([jnp.where(lane < HEAD_DIM, q, jnp.zeros_like(q)),
                              jnp.where(lane >= HEAD_DIM, q, jnp.zeros_like(q))], axis=0)
        gq = jnp.concatenate([jnp.sum(jnp.where(lane == 2 * p + hh, g_own, 0.0), axis=1, keepdims=True)
                              for hh in range(2)], axis=0)
        bias = jnp.where(second, grow[2 * p + 1:2 * p + 2, :], grow[2 * p:2 * p + 1, :]) - gq
        s_c = _dot_nt(q2, ck_ref[:, sl].astype(BF16)) + bias[:, 0:past]
        s_n = jnp.where(causal, _dot_nt(q2, knew[...]) + bias[:, past:s_pad], NEG_INF)
        m = jnp.maximum(jnp.max(s_c, axis=1, keepdims=True), jnp.max(s_n, axis=1, keepdims=True))
        e_c = jnp.exp(s_c - m)
        e_n = jnp.exp(s_n - m)
        l = jnp.sum(e_c, axis=1, keepdims=True) + jnp.sum(e_n, axis=1, keepdims=True)
        out = (_dot(e_c.astype(BF16), cv_ref[:, sl].astype(BF16)) + _dot(e_n.astype(BF16), vnew[...])) / l
        o_ref[:, sl] = jnp.where(lane < HEAD_DIM, out[0:length], out[length:2 * length]).astype(BF16)


def _decode(q, kn, vn, cache_k, cache_v, cache_logf, lfp, batch, length, j):
    rows = q.shape[0]
    past = cache_k.shape[2]
    s_pad = past + LANES
    ck = cache_k.reshape(cache_k.shape[0] * batch, past, W_C)
    cv = cache_v.reshape(cache_v.shape[0] * batch, past, W_C)
    cl = cache_logf.reshape(cache_logf.shape[0] * batch, past, N_HEADS)
    new_map = lambda b: (b, 0)
    cache_map = lambda b: (j * batch + b, 0, 0)
    return pl.pallas_call(
        functools.partial(_decode_kernel, past=past, length=length, s_pad=s_pad),
        grid=(batch,),
        in_specs=[
            pl.BlockSpec((length, W_C), new_map),
            pl.BlockSpec((length, W_C), new_map),
            pl.BlockSpec((length, W_C), new_map),
            pl.BlockSpec((None, past, W_C), cache_map),
            pl.BlockSpec((None, past, W_C), cache_map),
            pl.BlockSpec((None, past, N_HEADS), cache_map),
            pl.BlockSpec((length, LANES), new_map),
        ],
        out_specs=pl.BlockSpec((length, W_C), new_map),
        out_shape=jax.ShapeDtypeStruct((rows, W_C), BF16),
        scratch_shapes=[
            pltpu.VMEM((LANES, LANES), BF16), pltpu.VMEM((LANES, LANES), BF16),
            pltpu.VMEM((s_pad, LANES), F32), pltpu.VMEM((s_pad, LANES), F32), pltpu.VMEM((LANES, s_pad), F32),
        ],
        compiler_params=_params(("arbitrary",)),
        name="decode",
    )(q, kn, vn, ck, cv, cl, lfp)


def _odd_out_kernel(o_ref, u_ref, x_ref, pp_ref, pw_ref, ps_ref, wout_ref, y_ref, np_ref, ext, *, nb, lt, n_l, pos0):
    j = pl.program_id(1)
    hist = POOL_MAX - 1

    @pl.when(j == 0)
    def _():
        ext[:, HALO_P - hist:HALO_P, :] = pp_ref[...]

    if n_l > 1:
        @pl.when(j > 0)
        def _():
            ext[:, 0:HALO_P, :] = ext[:, lt:lt + HALO_P, :]

    u = u_ref[...].reshape(nb, lt, W_D)
    ext[:, HALO_P:HALO_P + lt, :] = u
    group = lax.broadcasted_iota(jnp.int32, (nb, lt, W_D), 2) // GROUP_DIM
    acc = u
    win = None
    for back in range(1, POOL_MAX):
        acc = acc + ext[:, HALO_P - back:HALO_P - back + lt, :]
        if back + 1 in POOL_WINDOWS:
            g = POOL_WINDOWS.index(back + 1)
            win = acc if win is None else jnp.where(group >= g, acc, win)
    width = jnp.left_shift(2, group)
    pos = pos0 + j * lt + lax.broadcasted_iota(jnp.int32, (nb, lt, W_D), 1)
    cnt = jnp.minimum(width, pos + 1).astype(F32)
    z = (win / cnt - u).reshape(nb * lt, W_D)
    yd = _dot(z.astype(BF16), pw_ref[...]) * ps_ref[...]
    y = _dot(o_ref[...], wout_ref[0:W_C, :]) + _dot(yd.astype(BF16), wout_ref[W_C:W_C + W_D, :])
    y_ref[...] = x_ref[...] + y

    @pl.when(j == n_l - 1)
    def _():
        np_ref[...] = ext[:, HALO_P + lt - hist:HALO_P + lt, :]


def _odd_out(o, u, x, batch, length, pos0, prev_pool, pool_w, pool_scale, w_out, j):
    rows, d = x.shape
    nb, lt, n_l = _seq_tiling(batch, length)
    tm = nb * lt
    row_map = lambda b, t: (b * n_l + t, 0)
    lay3 = lambda b, t: (j, 0, 0)
    seq_map = lambda b, t: (b, 0, 0)
    return pl.pallas_call(
        functools.partial(_odd_out_kernel, nb=nb, lt=lt, n_l=n_l, pos0=pos0),
        grid=(batch // nb, n_l),
        in_specs=[
            pl.BlockSpec((tm, W_C), row_map),
            pl.BlockSpec((tm, W_D), row_map),
            pl.BlockSpec((tm, d), row_map),
            pl.BlockSpec((nb, POOL_MAX - 1, W_D), seq_map),
            pl.BlockSpec((None, W_D, W_D), lay3),
            pl.BlockSpec((None, 1, W_D), lay3),
            pl.BlockSpec((None, W_C + W_D, d), lay3),
        ],
        out_specs=[pl.BlockSpec((tm, d), row_map), pl.BlockSpec((nb, POOL_MAX - 1, W_D), seq_map)],
        out_shape=[jax.ShapeDtypeStruct((rows, d), F32), jax.ShapeDtypeStruct((batch, POOL_MAX - 1, W_D), F32)],
        scratch_shapes=[pltpu.VMEM((nb, HALO_P + lt, W_D), F32)],
        compiler_params=_params(("arbitrary", "arbitrary")),
        name="odd_out",
    )(o, u, x, prev_pool, pool_w, pool_scale, w_out)


def _mixer(x, i, batch, length, prev_a, prev_b, cache, prev_pool, W, outs):
    j = i // 2
    if i % 2 == 0:
        x, sa, sb = _even(x, batch, length, W['norm_mix'], W['w_in_even'], W['conv_a_w'], W['conv_b_w'],
                          W['conv_b_bias'], W['ln_b_gain'], W['ln_b_bias'], W['w_out_even'],
                          prev_a[j], prev_b[j], i, j)
        outs['a'].append(sa)
        outs['b'].append(sb)
        return x
    res = _odd_in(x, W['norm_mix'], W['w_in_odd'], W['b_forget'], i, j, prompt=cache is None)
    if cache is None:
        qa, ka, vt, k, v, lf, u = res
        o = _flash(qa, ka, vt)
        pos0 = 0
    else:
        q, kb, vb, k, v, lf, lfp, u = res
        o = _decode(q, kb, vb, cache[0], cache[1], cache[2], lfp, batch, length, j)
        pos0 = cache[0].shape[2]
    x, sp = _odd_out(o, u, x, batch, length, pos0, prev_pool[j], W['pool_w'], W['pool_scale'], W['w_out_odd'], j)
    outs['k'].append(k.reshape(batch, length, N_HEADS, HEAD_DIM))
    outs['v'].append(v.reshape(batch, length, N_HEADS, HEAD_DIM))
    outs['f'].append(lf.reshape(batch, length, N_HEADS))
    outs['p'].append(sp)
    return x


def _trunks(groups, W):
    depth = W['w_ffn1_in'].shape[0]
    shapes = [g[0].shape for g in groups]
    xs = [g[0].reshape(s[0] * s[1], s[2]) for g, s in zip(groups, shapes)]
    outs = [{n: [] for n in 'abkvfp'} for _ in groups]
    for i in range(depth):
        xs = _ffn(xs, W['norm_ffn1'], W['w_ffn1_in'], W['w_ffn1_out'], i)
        xs = [_mixer(x, i, s[0], s[1], g[1], g[2], g[3], g[4], W, o) for x, s, g, o in zip(xs, shapes, groups, outs)]
        xs = _ffn(xs, W['norm_ffn2'], W['w_ffn2_in'], W['w_ffn2_out'], i,
                  g_final=W['norm_final'] if i == depth - 1 else None)
    return [(x.reshape(s),) + tuple(jnp.stack(o[n]) for n in 'abkvfp') for x, s, o in zip(xs, shapes, outs)]


def kernel(x_prompt, x_sample, state_conv_a, state_conv_b, cache_k, cache_v, cache_logf, state_pool, norm_ffn1, w_ffn1_in, w_ffn1_out, norm_mix, w_in_even, conv_a_w, conv_b_w, conv_b_bias, ln_b_gain, ln_b_bias, w_out_even, w_in_odd, b_forget, pool_w, pool_scale, w_out_odd, norm_ffn2, w_ffn2_in, w_ffn2_out, norm_final):
    depth, d = norm_ffn1.shape
    n_odd = w_in_odd.shape[0]
    w_odd = jnp.concatenate([
        w_in_odd[:, :, :3 * W_C],
        jnp.pad(w_in_odd[:, :, 3 * W_C:3 * W_C + N_HEADS], ((0, 0), (0, 0), (0, LANES - N_HEADS))),
        w_in_odd[:, :, 3 * W_C + N_HEADS:],
    ], axis=-1).astype(BF16)
    pool_bd = jnp.zeros((n_odd, W_D, W_D), F32)
    for g in range(len(POOL_WINDOWS)):
        sl = slice(g * GROUP_DIM, (g + 1) * GROUP_DIM)
        pool_bd = pool_bd.at[:, sl, sl].set(pool_w[:, g])
    W = {
        'norm_ffn1': norm_ffn1.reshape(depth, 1, d), 'norm_ffn2': norm_ffn2.reshape(depth, 1, d),
        'norm_mix': norm_mix.reshape(depth, 1, d), 'norm_final': norm_final.reshape(1, d),
        'w_ffn1_in': w_ffn1_in.astype(BF16), 'w_ffn1_out': w_ffn1_out.astype(BF16),
        'w_ffn2_in': w_ffn2_in.astype(BF16), 'w_ffn2_out': w_ffn2_out.astype(BF16),
        'w_in_even': w_in_even.astype(BF16), 'w_out_even': w_out_even.astype(BF16),
        'conv_a_w': conv_a_w, 'conv_b_w': conv_b_w,
        'conv_b_bias': conv_b_bias[:, None, :], 'ln_b_gain': ln_b_gain[:, None, :], 'ln_b_bias': ln_b_bias[:, None, :],
        'w_in_odd': w_odd, 'w_out_odd': w_out_odd.astype(BF16),
        'b_forget': jnp.pad(b_forget, ((0, 0), (0, LANES - N_HEADS)))[:, None, :],
        'pool_w': pool_bd.astype(BF16), 'pool_scale': pool_scale[:, None, :],
    }
    b_p = x_prompt.shape[0]
    zeros = lambda n, r, c: jnp.zeros((n, b_p, r, c), x_prompt.dtype)
    n_even = w_in_even.shape[0]
    prompt, sample = _trunks([
        (x_prompt, zeros(n_even, K_A - 1, W_A), zeros(n_even, K_B - 1, W_B), None, zeros(n_odd, POOL_MAX - 1, W_D)),
        (x_sample, state_conv_a, state_conv_b, (cache_k, cache_v, cache_logf), state_pool),
    ], W)
    return (prompt[0], sample[0]) + prompt[1:] + sample[1:]
```

```python
import functools

import numpy as np
import jax
import jax.numpy as jnp
from jax import lax
from jax.experimental import pallas as pl
from jax.experimental.pallas import tpu as pltpu

F32 = jnp.float32
BF16 = jnp.bfloat16

EPS = 1e-6
NEG_INF = -1e30
GROUP_DIM = 64
HEAD_DIM = 64
N_HEADS = 12
W_A = 512
W_B = 512
K_A = 3
K_B = 31
W_C = N_HEADS * HEAD_DIM
W_D = 256
POOL_WINDOWS = (2, 4, 8, 16)
POOL_MAX = 16
LANES = 128
SUBLANES = 8
BF16_SUBLANES = 16
V_ROWS = HEAD_DIM + BF16_SUBLANES
HALO_A = 8
HALO_B = 32
HALO_P = 16
VMEM_LIMIT = 56 * 1024 * 1024

ROW_TILE = 512
FFN_ROW_TILE = 1024
ATT_KEY_TILE = 512
ATT_QUERY_TILE = 512
FF_CHUNK = 256
SCAN_BLOCK = 128
FLASH_GROUP = 2
LOG2E = 1.4426950408889634


def _rms(x, g):
    return (x * lax.rsqrt(jnp.mean(x * x, axis=-1, keepdims=True) + EPS)) * g


def _dot(a, b):
    return jnp.dot(a, b, preferred_element_type=F32)


def _dot_nt(a, b):
    return lax.dot_general(a, b, (((1,), (1,)), ((), ())), preferred_element_type=F32)


def _split3(f):
    hi = f.astype(BF16)
    r1 = f - hi.astype(F32)
    mid = r1.astype(BF16)
    lo = (r1 - mid.astype(F32)).astype(BF16)
    return hi, mid, lo


def _dot_ones(a, b):
    hi, mid, lo = _split3(b)
    return _dot(a, hi) + _dot(a, mid) + _dot(a, lo)


def _params(semantics):
    return pltpu.CompilerParams(dimension_semantics=semantics, vmem_limit_bytes=VMEM_LIMIT)


def _ffn_kernel(*refs, n_chunks, final, tiles):
    n = len(tiles)
    x_refs = refs[:n]
    g_ref, wg_ref, wu_ref, wo_ref = refs[n:n + 4]
    gf_ref = refs[n + 4] if final else None
    o_refs = refs[n + 4 + final:2 * n + 4 + final]
    h_scr, acc_scr = refs[2 * n + 4 + final:]

    def body(x_ref, o_ref, tm):
        h_scr[0:tm, :] = _rms(x_ref[...], g_ref[...]).astype(BF16)
        for c in range(n_chunks):
            sl = slice(c * FF_CHUNK, (c + 1) * FF_CHUNK)
            hb = h_scr[0:tm, :]
            gate = _dot(hb, wg_ref[:, sl])
            up = _dot(hb, wu_ref[:, sl])
            act = ((gate * jax.nn.sigmoid(gate)) * up).astype(BF16)
            part = _dot(act, wo_ref[sl, :])
            if c == 0:
                acc_scr[0:tm, :] = part
            else:
                acc_scr[0:tm, :] += part
        y = x_ref[...] + 0.5 * acc_scr[0:tm, :]
        if final:
            y = _rms(y, gf_ref[...])
        o_ref[...] = y

    i = pl.program_id(0)
    start = 0
    for x_ref, o_ref, (tm, count) in zip(x_refs, o_refs, tiles):
        if n == 1:
            body(x_ref, o_ref, tm)
        else:
            pl.when(jnp.logical_and(i >= start, i < start + count))(functools.partial(body, x_ref, o_ref, tm))
        start += count


def _ffn(xs, g, w_in, w_out, layer, g_final=None):
    d = xs[0].shape[1]
    d_ff = w_out.shape[1]
    final = g_final is not None
    tiles, row_specs, start = [], [], 0
    for x in xs:
        tm = min(x.shape[0], FFN_ROW_TILE, max(ROW_TILE, x.shape[0] // 2))
        count = x.shape[0] // tm
        tiles.append((tm, count))
        row_specs.append(pl.BlockSpec((tm, d), lambda i, s=start, c=count: (jnp.clip(i - s, 0, c - 1), 0)))
        start += count
    in_specs = row_specs + [
        pl.BlockSpec((None, 1, d), lambda i: (layer, 0, 0)),
        pl.BlockSpec((None, d, d_ff), lambda i: (layer, 0, 0), pipeline_mode=pl.Buffered(1)),
        pl.BlockSpec((None, d, d_ff), lambda i: (layer, 0, 1), pipeline_mode=pl.Buffered(1)),
        pl.BlockSpec((None, d_ff, d), lambda i: (layer, 0, 0), pipeline_mode=pl.Buffered(1)),
    ]
    args = list(xs) + [g, w_in, w_in, w_out]
    if final:
        in_specs.append(pl.BlockSpec((1, d), lambda i: (0, 0)))
        args.append(g_final)
    tm_max = max(tm for tm, _ in tiles)
    return pl.pallas_call(
        functools.partial(_ffn_kernel, n_chunks=d_ff // FF_CHUNK, final=final, tiles=tuple(tiles)),
        grid=(start,),
        in_specs=in_specs,
        out_specs=row_specs,
        out_shape=[jax.ShapeDtypeStruct(x.shape, F32) for x in xs],
        scratch_shapes=[pltpu.VMEM((tm_max, d), BF16), pltpu.VMEM((tm_max, d), F32)],
        compiler_params=_params(("arbitrary",)),
        name="ffn",
    )(*args)


def _even_kernel(x_ref, g_ref, win_ref, wa_ref, wb_ref, bb_ref, lng_ref, lnb_ref, wout_ref, pa_ref, pb_ref,
                 o_ref, na_ref, nb_ref, ext_a, ext_b, shift_scr, *, nb, lt, n_l):
    j = pl.program_id(1)

    @pl.when(j == 0)
    def _():
        ext_a[:, HALO_A - (K_A - 1):HALO_A, :] = pa_ref[...]
        ext_b[:, HALO_B - (K_B - 1):HALO_B, :] = pb_ref[...]

    if n_l > 1:
        @pl.when(j > 0)
        def _():
            ext_a[:, 0:HALO_A, :] = ext_a[:, lt:lt + HALO_A, :]
            ext_b[:, 0:HALO_B, :] = ext_b[:, lt:lt + HALO_B, :]

    x = x_ref[...]
    h = _rms(x, g_ref[...]).astype(BF16)

    def proj(i, width):
        return _dot(h, win_ref[:, i * width:(i + 1) * width])

    ext_a[:, HALO_A:HALO_A + lt, :] = (proj(1, W_A) * proj(2, W_A)).reshape(nb, lt, W_A)
    ext_b[:, HALO_B:HALO_B + lt, :] = (proj(3, W_B) * jax.nn.sigmoid(proj(4, W_B))).reshape(nb, lt, W_B)

    def conv(ext, w_ref, taps, halo):
        base = halo - (taps - 1)
        acc = None
        for r in range(SUBLANES):
            offs = [base + k for k in range(taps) if (base + k) % SUBLANES == r]
            if not offs:
                continue
            if len(offs) == 1 or r == 0:
                windows = [ext[:, off:off + lt, :] for off in offs]
            else:
                buf = shift_scr.at[r % 2]
                span = offs[-1] - offs[0] + lt
                buf[:, 0:span, :] = ext[:, offs[0]:offs[0] + span, :]
                windows = [buf[:, off - offs[0]:off - offs[0] + lt, :] for off in offs]
            for off, win in zip(offs, windows):
                term = w_ref[off - base:off - base + 1, :] * win
                acc = term if acc is None else acc + term
        return acc.reshape(nb * lt, acc.shape[-1])

    ya = proj(0, W_A) * conv(ext_a, wa_ref, K_A, HALO_A)
    yc = conv(ext_b, wb_ref, K_B, HALO_B) + bb_ref[...]
    mu = jnp.mean(yc, axis=-1, keepdims=True)
    dev = yc - mu
    var = jnp.mean(dev * dev, axis=-1, keepdims=True)
    yn = dev * lax.rsqrt(var + EPS) * lng_ref[...] + lnb_ref[...]
    yb = yn * jax.nn.sigmoid(yn)
    y = _dot(ya.astype(BF16), wout_ref[0:W_A, :]) + _dot(yb.astype(BF16), wout_ref[W_A:W_A + W_B, :])
    o_ref[...] = x + y

    @pl.when(j == n_l - 1)
    def _():
        na_ref[...] = ext_a[:, HALO_A + lt - (K_A - 1):HALO_A + lt, :]
        nb_ref[...] = ext_b[:, HALO_B + lt - (K_B - 1):HALO_B + lt, :]


def _seq_tiling(batch, length):
    if length >= ROW_TILE:
        return 1, ROW_TILE, length // ROW_TILE
    nb = max(1, min(batch, ROW_TILE // length))
    return nb, length, 1


def _even(x, batch, length, g, w_in, conv_a, conv_b, bias_b, ln_g, ln_b, w_out, prev_a, prev_b, layer, j):
    rows, d = x.shape
    nb, lt, n_l = _seq_tiling(batch, length)
    tm = nb * lt
    d_in = w_in.shape[-1]
    row_map = lambda b, t: (b * n_l + t, 0)
    lay3 = lambda b, t: (j, 0, 0)
    seq_map = lambda b, t: (b, 0, 0)
    return pl.pallas_call(
        functools.partial(_even_kernel, nb=nb, lt=lt, n_l=n_l),
        grid=(batch // nb, n_l),
        in_specs=[
            pl.BlockSpec((tm, d), row_map),
            pl.BlockSpec((None, 1, d), lambda b, t: (layer, 0, 0)),
            pl.BlockSpec((None, d, d_in), lay3),
            pl.BlockSpec((None, K_A, W_A), lay3),
            pl.BlockSpec((None, K_B, W_B), lay3),
            pl.BlockSpec((None, 1, W_B), lay3),
            pl.BlockSpec((None, 1, W_B), lay3),
            pl.BlockSpec((None, 1, W_B), lay3),
            pl.BlockSpec((None, W_A + W_B, d), lay3),
            pl.BlockSpec((nb, K_A - 1, W_A), seq_map),
            pl.BlockSpec((nb, K_B - 1, W_B), seq_map),
        ],
        out_specs=[
            pl.BlockSpec((tm, d), row_map),
            pl.BlockSpec((nb, K_A - 1, W_A), seq_map),
            pl.BlockSpec((nb, K_B - 1, W_B), seq_map),
        ],
        out_shape=[
            jax.ShapeDtypeStruct((rows, d), F32),
            jax.ShapeDtypeStruct((batch, K_A - 1, W_A), F32),
            jax.ShapeDtypeStruct((batch, K_B - 1, W_B), F32),
        ],
        scratch_shapes=[pltpu.VMEM((nb, HALO_A + lt, W_A), F32), pltpu.VMEM((nb, HALO_B + lt, W_B), F32),
                        pltpu.VMEM((2, nb, HALO_B + lt, W_B), F32)],
        compiler_params=_params(("arbitrary", "arbitrary")),
        name="even",
    )(x, g, w_in, conv_a, conv_b, bias_b, ln_g, ln_b, w_out, prev_a, prev_b)


def _odd_in_kernel(*refs, lt, prompt):
    x_ref, g_ref, w_ref, bf_ref = refs[:4]
    if prompt:
        pq_ref, pk_ref, oq_ref, ok_ref = refs[4:8]
        qa_ref, ka_ref, vt_ref, k_ref, v_ref, lf_ref, u_ref, carry = refs[8:]
    else:
        q_ref, kb_ref, vb_ref, k_ref, v_ref, lf_ref, lfp_ref, u_ref = refs[4:]
    h = _rms(x_ref[...], g_ref[...]).astype(BF16)
    q_scale = HEAD_DIM ** -0.5 * (LOG2E if prompt else 1.0)
    q = (_dot(h, w_ref[:, 0:W_C]) * q_scale).astype(BF16)
    k = _dot(h, w_ref[:, W_C:2 * W_C])
    k_ref[...] = k
    kb = k.astype(BF16)
    v = _dot(h, w_ref[:, 2 * W_C:3 * W_C])
    v_ref[...] = v
    z = _dot(h, w_ref[:, 3 * W_C:3 * W_C + LANES]) + bf_ref[...]
    lf = jnp.minimum(z, 0.0) - jnp.log1p(jnp.exp(-jnp.abs(z)))
    lf_ref[...] = lf[:, :N_HEADS]
    u_ref[...] = _dot(h, w_ref[:, 3 * W_C + LANES:3 * W_C + LANES + W_D])
    if not prompt:
        q_ref[...] = q
        kb_ref[...] = kb
        vb_ref[...] = v.astype(BF16)
        lfp_ref[...] = lf
        return

    @pl.when(pl.program_id(0) == 0)
    def _():
        carry[...] = jnp.zeros_like(carry)

    r = lax.broadcasted_iota(jnp.int32, (lt, lt), 0)
    c = lax.broadcasted_iota(jnp.int32, (lt, lt), 1)
    f = _dot_ones(jnp.where(c <= r, 1.0, 0.0).astype(BF16), lf) + carry[...]
    carry[...] = f[lt - 1:lt, :]
    fa = jnp.concatenate(_split3(f * LOG2E), axis=1)
    aug_q = (_dot(fa, pq_ref[...]) + oq_ref[...]).astype(BF16)
    aug_k = (_dot(fa, pk_ref[...]) + ok_ref[...]).astype(BF16)
    vt = v.T.astype(BF16)
    upper_half = lax.broadcasted_iota(jnp.int32, (1, LANES), 1) >= HEAD_DIM
    ones = jnp.ones((BF16_SUBLANES, lt), BF16)
    for p in range(N_HEADS // 2):
        sl = slice(p * LANES, (p + 1) * LANES)
        ka_ref[p, :, 0:LANES] = kb[:, sl]
        ka_ref[p, :, LANES:2 * LANES] = aug_k[:, sl]
        for hh in range(2):
            h = 2 * p + hh
            keep = upper_half == (hh == 1)
            qa_ref[h, :, 0:LANES] = jnp.where(keep, q[:, sl], jnp.zeros_like(q[:, sl]))
            qa_ref[h, :, LANES:2 * LANES] = jnp.where(keep, aug_q[:, sl], jnp.zeros_like(aug_q[:, sl]))
            vt_ref[h, 0:HEAD_DIM, :] = vt[h * HEAD_DIM:(h + 1) * HEAD_DIM, :]
            vt_ref[h, HEAD_DIM:V_ROWS, :] = ones


def _placement():
    pq = np.zeros((3 * LANES, W_C), np.float32)
    pk = np.zeros((3 * LANES, W_C), np.float32)
    oq = np.zeros((1, W_C), np.float32)
    ok = np.zeros((1, W_C), np.float32)
    for h in range(N_HEADS):
        base = (h // 2) * LANES + (h % 2) * HEAD_DIM
        for part in range(3):
            pq[part * LANES + h, base + part] = 1.0
            pk[part * LANES + h, base + 3 + part] = -1.0
            ok[0, base + part] = 1.0
            oq[0, base + 3 + part] = 1.0
    return jnp.asarray(pq, BF16), jnp.asarray(pk, BF16), jnp.asarray(oq), jnp.asarray(ok)


def _odd_in(x, g, w, b_f, layer, j, prompt):
    rows, d = x.shape
    tm = min(ROW_TILE, rows)
    n = rows // tm
    d_in = w.shape[-1]
    n_pairs = N_HEADS // 2
    row = lambda i: (i, 0)
    lay3 = lambda i: (j, 0, 0)
    const = lambda i: (0, 0)
    in_specs = [
        pl.BlockSpec((tm, d), row),
        pl.BlockSpec((None, 1, d), lambda i: (layer, 0, 0)),
        pl.BlockSpec((None, d, d_in), lay3),
        pl.BlockSpec((None, 1, LANES), lay3),
    ]
    args = [x, g, w, b_f]
    common_specs = [pl.BlockSpec((tm, W_C), row), pl.BlockSpec((tm, W_C), row), pl.BlockSpec((tm, N_HEADS), row)]
    common_shape = [jax.ShapeDtypeStruct((rows, W_C), F32), jax.ShapeDtypeStruct((rows, W_C), F32),
                    jax.ShapeDtypeStruct((rows, N_HEADS), F32)]
    u_spec, u_shape = pl.BlockSpec((tm, W_D), row), jax.ShapeDtypeStruct((rows, W_D), F32)
    if prompt:
        in_specs += [pl.BlockSpec((3 * LANES, W_C), const), pl.BlockSpec((3 * LANES, W_C), const),
                     pl.BlockSpec((1, W_C), const), pl.BlockSpec((1, W_C), const)]
        args += list(_placement())
        out_specs = [pl.BlockSpec((N_HEADS, tm, 2 * LANES), lambda i: (0, i, 0)),
                     pl.BlockSpec((n_pairs, tm, 2 * LANES), lambda i: (0, i, 0)),
                     pl.BlockSpec((N_HEADS, V_ROWS, tm), lambda i: (0, 0, i))] + common_specs + [u_spec]
        out_shape = [jax.ShapeDtypeStruct((N_HEADS, rows, 2 * LANES), BF16),
                     jax.ShapeDtypeStruct((n_pairs, rows, 2 * LANES), BF16),
                     jax.ShapeDtypeStruct((N_HEADS, V_ROWS, rows), BF16)] + common_shape + [u_shape]
        scratch = [pltpu.VMEM((1, LANES), F32)]
    else:
        out_specs = [pl.BlockSpec((tm, W_C), row)] * 3 + common_specs + [pl.BlockSpec((tm, LANES), row), u_spec]
        out_shape = [jax.ShapeDtypeStruct((rows, W_C), BF16)] * 3 + common_shape + [
            jax.ShapeDtypeStruct((rows, LANES), F32), u_shape]
        scratch = []
    return pl.pallas_call(
        functools.partial(_odd_in_kernel, lt=tm, prompt=prompt),
        grid=(n,),
        in_specs=in_specs,
        out_specs=out_specs,
        out_shape=out_shape,
        scratch_shapes=scratch,
        compiler_params=_params(("arbitrary",)),
        name="odd_in",
    )(*args)


def _flash_kernel(qi_tab, ki_tab, qa_ref, ka_ref, vt_ref, o_ref, m_scr, acc_scr, *, tq, tk):
    t = pl.program_id(0)
    qi = qi_tab[t]
    ki = ki_tab[t]
    lead = qi * tq - ki * tk

    @pl.when(ki == 0)
    def _():
        m_scr[...] = jnp.full_like(m_scr, NEG_INF)
        acc_scr[...] = jnp.zeros_like(acc_scr)

    def head_scores(h):
        return _dot_nt(ka_ref[h // 2], qa_ref[h])

    def head_update(h, st, masked):
        if masked:
            key = lax.broadcasted_iota(jnp.int32, st.shape, 0)
            qry = lax.broadcasted_iota(jnp.int32, st.shape, 1)
            st = jnp.where(key <= (qry if tq == tk else qry + lead), st, NEG_INF)
        m_prev = m_scr[h]
        m_new = jnp.maximum(m_prev, jnp.max(st, axis=0, keepdims=True))
        alpha = jnp.exp2(m_prev - m_new)
        e = jnp.exp2(st - m_new).astype(BF16)
        acc_scr[h] = alpha * acc_scr[h] + _dot(vt_ref[h], e)
        m_scr[h] = m_new

    def step(masked):
        groups = [range(g, g + FLASH_GROUP) for g in range(0, N_HEADS, FLASH_GROUP)]
        scores = [head_scores(h) for h in groups[0]]
        for i, group in enumerate(groups):
            ahead = [head_scores(h) for h in groups[i + 1]] if i + 1 < len(groups) else None
            for h, st in zip(group, scores):
                head_update(h, st, masked)
            scores = ahead

    crosses_diagonal = lead < tk - 1

    @pl.when(jnp.logical_not(crosses_diagonal))
    def _():
        step(False)

    @pl.when(crosses_diagonal)
    def _():
        step(True)

    @pl.when(lead + tq == tk)
    def _():
        for p in range(N_HEADS // 2):
            pair_out = [acc_scr[h, 0:HEAD_DIM, :] / acc_scr[h, HEAD_DIM:HEAD_DIM + 1, :] for h in (2 * p, 2 * p + 1)]
            o_ref[:, p * LANES:(p + 1) * LANES] = jnp.concatenate(pair_out, axis=0).T.astype(BF16)


def _flash(qa, ka, vt):
    rows = qa.shape[1]
    n_pairs = N_HEADS // 2
    tk = min(ATT_KEY_TILE, rows)
    tq = min(ATT_QUERY_TILE, rows)
    nq = rows // tq
    k_per_q = tq // tk
    qi_tab = np.concatenate([np.full((i + 1) * k_per_q, i, np.int32) for i in range(nq)])
    ki_tab = np.concatenate([np.arange((i + 1) * k_per_q, dtype=np.int32) for i in range(nq)])
    grid_spec = pltpu.PrefetchScalarGridSpec(
        num_scalar_prefetch=2,
        grid=(qi_tab.shape[0],),
        in_specs=[
            pl.BlockSpec((N_HEADS, tq, 2 * LANES), lambda t, qt, kt: (0, qt[t], 0)),
            pl.BlockSpec((n_pairs, tk, 2 * LANES), lambda t, qt, kt: (0, kt[t], 0)),
            pl.BlockSpec((N_HEADS, V_ROWS, tk), lambda t, qt, kt: (0, 0, kt[t])),
        ],
        out_specs=pl.BlockSpec((tq, W_C), lambda t, qt, kt: (qt[t], 0)),
        scratch_shapes=[pltpu.VMEM((N_HEADS, 1, tq), F32), pltpu.VMEM((N_HEADS, V_ROWS, tq), F32)],
    )
    return pl.pallas_call(
        functools.partial(_flash_kernel, tq=tq, tk=tk),
        grid_spec=grid_spec,
        out_shape=jax.ShapeDtypeStruct((rows, W_C), BF16),
        compiler_params=_params(("arbitrary",)),
        name="flash",
    )(jnp.asarray(qi_tab), jnp.asarray(ki_tab), qa, ka, vt)


def _decode_kernel(q_ref, kn_ref, vn_ref, ck_ref, cv_ref, cl_ref, lfp_ref, o_ref, knew, vnew, lfa, gcol, grow,
                   *, past, length, s_pad):
    lane = lax.broadcasted_iota(jnp.int32, (1, LANES), 1)
    n_blk = s_pad // SCAN_BLOCK

    lfa[...] = jnp.zeros_like(lfa)
    lfa[0:past, 0:N_HEADS] = cl_ref[...]
    lfa[past:past + length, :] = lfp_ref[...]
    r = lax.broadcasted_iota(jnp.int32, (SCAN_BLOCK, SCAN_BLOCK), 0)
    c = lax.broadcasted_iota(jnp.int32, (SCAN_BLOCK, SCAN_BLOCK), 1)
    upper = jnp.where(c > r, 1.0, 0.0).astype(BF16)
    carry = jnp.zeros((1, LANES), F32)
    for b in reversed(range(n_blk)):
        blk = lfa[b * SCAN_BLOCK:(b + 1) * SCAN_BLOCK, :]
        g = _dot_ones(upper, blk) + carry
        carry = carry + jnp.sum(blk, axis=0, keepdims=True)
        gcol[b * SCAN_BLOCK:(b + 1) * SCAN_BLOCK, :] = g
        grow[:, b * SCAN_BLOCK:(b + 1) * SCAN_BLOCK] = g.T

    pad = jnp.zeros((LANES - length, LANES), BF16)
    second = lax.broadcasted_iota(jnp.int32, (2 * length, 1), 0) >= length
    row = lax.broadcasted_iota(jnp.int32, (2 * length, LANES), 0)
    col = lax.broadcasted_iota(jnp.int32, (2 * length, LANES), 1)
    causal = col <= jnp.where(row >= length, row - length, row)
    g_own = gcol[past:past + length, :]
    for p in range(N_HEADS // 2):
        sl = slice(p * LANES, (p + 1) * LANES)
        knew[0:length, :] = kn_ref[:, sl]
        knew[length:LANES, :] = pad
        vnew[0:length, :] = vn_ref[:, sl]
        vnew[length:LANES, :] = pad
        q = q_ref[:, sl]
        q2 = jnp.concatenate([jnp.where(lane < HEAD_DIM, q, jnp.zeros_like(q)),
                              jnp.where(lane >= HEAD_DIM, q, jnp.zeros_like(q))], axis=0)
        gq = jnp.concatenate([jnp.sum(jnp.where(lane == 2 * p + hh, g_own, 0.0), axis=1, keepdims=True)
                              for hh in range(2)], axis=0)
        bias = jnp.where(second, grow[2 * p + 1:2 * p + 2, :], grow[2 * p:2 * p + 1, :]) - gq
        s_c = _dot_nt(q2, ck_ref[:, sl].astype(BF16)) + bias[:, 0:past]
        s_n = jnp.where(causal, _dot_nt(q2, knew[...]) + bias[:, past:s_pad], NEG_INF)
        m = jnp.maximum(jnp.max(s_c, axis=1, keepdims=True), jnp.max(s_n, axis=1, keepdims=True))
        e_c = jnp.exp(s_c - m)
        e_n = jnp.exp(s_n - m)
        l = jnp.sum(e_c, axis=1, keepdims=True) + jnp.sum(e_n, axis=1, keepdims=True)
        out = (_dot(e_c.astype(BF16), cv_ref[:, sl].astype(BF16)) + _dot(e_n.astype(BF16), vnew[...])) / l
        o_ref[:, sl] = jnp.where(lane < HEAD_DIM, out[0:length], out[length:2 * length]).astype(BF16)


def _decode(q, kn, vn, cache_k, cache_v, cache_logf, lfp, batch, length, j):
    rows = q.shape[0]
    past = cache_k.shape[2]
    s_pad = past + LANES
    ck = cache_k.reshape(cache_k.shape[0] * batch, past, W_C)
    cv = cache_v.reshape(cache_v.shape[0] * batch, past, W_C)
    cl = cache_logf.reshape(cache_logf.shape[0] * batch, past, N_HEADS)
    new_map = lambda b: (b, 0)
    cache_map = lambda b: (j * batch + b, 0, 0)
    return pl.pallas_call(
        functools.partial(_decode_kernel, past=past, length=length, s_pad=s_pad),
        grid=(batch,),
        in_specs=[
            pl.BlockSpec((length, W_C), new_map),
            pl.BlockSpec((length, W_C), new_map),
            pl.BlockSpec((length, W_C), new_map),
            pl.BlockSpec((None, past, W_C), cache_map),
            pl.BlockSpec((None, past, W_C), cache_map),
            pl.BlockSpec((None, past, N_HEADS), cache_map),
            pl.BlockSpec((length, LANES), new_map),
        ],
        out_specs=pl.BlockSpec((length, W_C), new_map),
        out_shape=jax.ShapeDtypeStruct((rows, W_C), BF16),
        scratch_shapes=[
            pltpu.VMEM((LANES, LANES), BF16), pltpu.VMEM((LANES, LANES), BF16),
            pltpu.VMEM((s_pad, LANES), F32), pltpu.VMEM((s_pad, LANES), F32), pltpu.VMEM((LANES, s_pad), F32),
        ],
        compiler_params=_params(("arbitrary",)),
        name="decode",
    )(q, kn, vn, ck, cv, cl, lfp)


def _odd_out_kernel(o_ref, u_ref, x_ref, pp_ref, pw_ref, ps_ref, wout_ref, y_ref, np_ref, ext, *, nb, lt, n_l, pos0):
    j = pl.program_id(1)
    hist = POOL_MAX - 1

    @pl.when(j == 0)
    def _():
        ext[:, HALO_P - hist:HALO_P, :] = pp_ref[...]

    if n_l > 1:
        @pl.when(j > 0)
        def _():
            ext[:, 0:HALO_P, :] = ext[:, lt:lt + HALO_P, :]

    u = u_ref[...].reshape(nb, lt, W_D)
    ext[:, HALO_P:HALO_P + lt, :] = u
    group = lax.broadcasted_iota(jnp.int32, (nb, lt, W_D), 2) // GROUP_DIM
    acc = u
    win = None
    for back in range(1, POOL_MAX):
        acc = acc + ext[:, HALO_P - back:HALO_P - back + lt, :]
        if back + 1 in POOL_WINDOWS:
            g = POOL_WINDOWS.index(back + 1)
            win = acc if win is None else jnp.where(group >= g, acc, win)
    width = jnp.left_shift(2, group)
    pos = pos0 + j * lt + lax.broadcasted_iota(jnp.int32, (nb, lt, W_D), 1)
    cnt = jnp.minimum(width, pos + 1).astype(F32)
    z = (win / cnt - u).reshape(nb * lt, W_D)
    yd = _dot(z.astype(BF16), pw_ref[...]) * ps_ref[...]
    y = _dot(o_ref[...], wout_ref[0:W_C, :]) + _dot(yd.astype(BF16), wout_ref[W_C:W_C + W_D, :])
    y_ref[...] = x_ref[...] + y

    @pl.when(j == n_l - 1)
    def _():
        np_ref[...] = ext[:, HALO_P + lt - hist:HALO_P + lt, :]


def _odd_out(o, u, x, batch, length, pos0, prev_pool, pool_w, pool_scale, w_out, j):
    rows, d = x.shape
    nb, lt, n_l = _seq_tiling(batch, length)
    tm = nb * lt
    row_map = lambda b, t: (b * n_l + t, 0)
    lay3 = lambda b, t: (j, 0, 0)
    seq_map = lambda b, t: (b, 0, 0)
    return pl.pallas_call(
        functools.partial(_odd_out_kernel, nb=nb, lt=lt, n_l=n_l, pos0=pos0),
        grid=(batch // nb, n_l),
        in_specs=[
            pl.BlockSpec((tm, W_C), row_map),
            pl.BlockSpec((tm, W_D), row_map),
            pl.BlockSpec((tm, d), row_map),
            pl.BlockSpec((nb, POOL_MAX - 1, W_D), seq_map),
            pl.BlockSpec((None, W_D, W_D), lay3),
            pl.BlockSpec((None, 1, W_D), lay3),
            pl.BlockSpec((None, W_C + W_D, d), lay3),
        ],
        out_specs=[pl.BlockSpec((tm, d), row_map), pl.BlockSpec((nb, POOL_MAX - 1, W_D), seq_map)],
        out_shape=[jax.ShapeDtypeStruct((rows, d), F32), jax.ShapeDtypeStruct((batch, POOL_MAX - 1, W_D), F32)],
        scratch_shapes=[pltpu.VMEM((nb, HALO_P + lt, W_D), F32)],
        compiler_params=_params(("arbitrary", "arbitrary")),
        name="odd_out",
    )(o, u, x, prev_pool, pool_w, pool_scale, w_out)


def _mixer(x, i, batch, length, prev_a, prev_b, cache, prev_pool, W, outs):
    j = i // 2
    if i % 2 == 0:
        x, sa, sb = _even(x, batch, length, W['norm_mix'], W['w_in_even'], W['conv_a_w'], W['conv_b_w'],
                          W['conv_b_bias'], W['ln_b_gain'], W['ln_b_bias'], W['w_out_even'],
                          prev_a[j], prev_b[j], i, j)
        outs['a'].append(sa)
        outs['b'].append(sb)
        return x
    res = _odd_in(x, W['norm_mix'], W['w_in_odd'], W['b_forget'], i, j, prompt=cache is None)
    if cache is None:
        qa, ka, vt, k, v, lf, u = res
        o = _flash(qa, ka, vt)
        pos0 = 0
    else:
        q, kb, vb, k, v, lf, lfp, u = res
        o = _decode(q, kb, vb, cache[0], cache[1], cache[2], lfp, batch, length, j)
        pos0 = cache[0].shape[2]
    x, sp = _odd_out(o, u, x, batch, length, pos0, prev_pool[j], W['pool_w'], W['pool_scale'], W['w_out_odd'], j)
    outs['k'].append(k.reshape(batch, length, N_HEADS, HEAD_DIM))
    outs['v'].append(v.reshape(batch, length, N_HEADS, HEAD_DIM))
    outs['f'].append(lf.reshape(batch, length, N_HEADS))
    outs['p'].append(sp)
    return x


def _trunks(groups, W):
    depth = W['w_ffn1_in'].shape[0]
    shapes = [g[0].shape for g in groups]
    xs = [g[0].reshape(s[0] * s[1], s[2]) for g, s in zip(groups, shapes)]
    outs = [{n: [] for n in 'abkvfp'} for _ in groups]
    for i in range(depth):
        xs = _ffn(xs, W['norm_ffn1'], W['w_ffn1_in'], W['w_ffn1_out'], i)
        xs = [_mixer(x, i, s[0], s[1], g[1], g[2], g[3], g[4], W, o) for x, s, g, o in zip(xs, shapes, groups, outs)]
        xs = _ffn(xs, W['norm_ffn2'], W['w_ffn2_in'], W['w_ffn2_out'], i,
                  g_final=W['norm_final'] if i == depth - 1 else None)
    return [(x.reshape(s),) + tuple(jnp.stack(o[n]) for n in 'abkvfp') for x, s, o in zip(xs, shapes, outs)]


def kernel(x_prompt, x_sample, state_conv_a, state_conv_b, cache_k, cache_v, cache_logf, state_pool, norm_ffn1, w_ffn1_in, w_ffn1_out, norm_mix, w_in_even, conv_a_w, conv_b_w, conv_b_bias, ln_b_gain, ln_b_bias, w_out_even, w_in_odd, b_forget, pool_w, pool_scale, w_out_odd, norm_ffn2, w_ffn2_in, w_ffn2_out, norm_final):
    depth, d = norm_ffn1.shape
    n_odd = w_in_odd.shape[0]
    w_odd = jnp.concatenate([
        w_in_odd[:, :, :3 * W_C],
        jnp.pad(w_in_odd[:, :, 3 * W_C:3 * W_C + N_HEADS], ((0, 0), (0, 0), (0, LANES - N_HEADS))),
        w_in_odd[:, :, 3 * W_C + N_HEADS:],
    ], axis=-1).astype(BF16)
    pool_bd = jnp.zeros((n_odd, W_D, W_D), F32)
    for g in range(len(POOL_WINDOWS)):
        sl = slice(g * GROUP_DIM, (g + 1) * GROUP_DIM)
        pool_bd = pool_bd.at[:, sl, sl].set(pool_w[:, g])
    W = {
        'norm_ffn1': norm_ffn1.reshape(depth, 1, d), 'norm_ffn2': norm_ffn2.reshape(depth, 1, d),
        'norm_mix': norm_mix.reshape(depth, 1, d), 'norm_final': norm_final.reshape(1, d),
        'w_ffn1_in': w_ffn1_in.astype(BF16), 'w_ffn1_out': w_ffn1_out.astype(BF16),
        'w_ffn2_in': w_ffn2_in.astype(BF16), 'w_ffn2_out': w_ffn2_out.astype(BF16),
        'w_in_even': w_in_even.astype(BF16), 'w_out_even': w_out_even.astype(BF16),
        'conv_a_w': conv_a_w, 'conv_b_w': conv_b_w,
        'conv_b_bias': conv_b_bias[:, None, :], 'ln_b_gain': ln_b_gain[:, None, :], 'ln_b_bias': ln_b_bias[:, None, :],
        'w_in_odd': w_odd, 'w_out_odd': w_out_odd.astype(BF16),
        'b_forget': jnp.pad(b_forget, ((0, 0), (0, LANES - N_HEADS)))[:, None, :],
        'pool_w': pool_bd.astype(BF16), 'pool_scale': pool_scale[:, None, :],
    }
    b_p = x_prompt.shape[0]
    zeros = lambda n, r, c: jnp.zeros((n, b_p, r, c), x_prompt.dtype)
    n_even = w_in_even.shape[0]
    prompt, sample = _trunks([
        (x_prompt, zeros(n_even, K_A - 1, W_A), zeros(n_even, K_B - 1, W_B), None, zeros(n_odd, POOL_MAX - 1, W_D)),
        (x_sample, state_conv_a, state_conv_b, (cache_k, cache_v, cache_logf), state_pool),
    ], W)
    return (prompt[0], sample[0]) + prompt[1:] + sample[1:]
```
